```python
import math
import jax
import jax.numpy as jnp
from jax import lax
import numpy as np

D_MODEL = 1024
BATCH = 16
SEQ = 2048
DEPTH = 4
DEC_BATCH = 128
DEC_SEQ = 8
PAST_LEN = 8192
PAGE_SIZE = 128

N_A_LAYERS = DEPTH // 2
N_B_LAYERS = DEPTH - N_A_LAYERS
CONV_WIDTH = 31
CONV_STATE = CONV_WIDTH - 1
N_HEADS = 8
QK_NOPE = 128
QK_ROPE = 64
V_HEAD = 128
KV_LORA = 256
Q_LORA = 512
ROPE_BASE = 10000.0
Q_BLOCK = 128
SCORE_SCALE = (QK_NOPE + QK_ROPE) ** -0.5
N_EXPERTS = 16
N_GROUPS = 4
EXPERTS_PER_GROUP = N_EXPERTS // N_GROUPS
TOP_K = 2
D_EXPERT = 512
ALPHA = (2 * DEPTH) ** 0.25
BETA = (8 * DEPTH) ** -0.25
LN_EPS = 1e-5
RMS_EPS = 1e-6

kernel_name = 'yoco_conformer_mla_groupmoe_step'


def layer_norm(x, g, b):
    xf = x.astype(jnp.float32)
    mu = jnp.mean(xf, axis=-1, keepdims=True)
    var = jnp.mean(jnp.square(xf - mu), axis=-1, keepdims=True)
    y = (xf - mu) * lax.rsqrt(var + LN_EPS)
    return (y * g.astype(jnp.float32) + b.astype(jnp.float32)).astype(x.dtype)


def rms_norm(x, g):
    xf = x.astype(jnp.float32)
    y = xf * lax.rsqrt(jnp.mean(jnp.square(xf), axis=-1, keepdims=True) + RMS_EPS)
    return (y * g.astype(jnp.float32)).astype(x.dtype)


def rope(x, pos):
    half = QK_ROPE // 2
    inv_freq = ROPE_BASE ** (-jnp.arange(half, dtype=jnp.float32) / half)
    ang = pos.astype(jnp.float32)[:, None] * inv_freq[None, :]
    cos = jnp.cos(ang)[:, None, :]
    sin = jnp.sin(ang)[:, None, :]
    xf = x.astype(jnp.float32)
    x1, x2 = xf[..., :half], xf[..., half:]
    return jnp.concatenate([x1 * cos - x2 * sin, x2 * cos + x1 * sin], axis=-1).astype(x.dtype)


def conv_module(x, past, w_pw1, b_pw1, w_dw, b_dw, g_cn, b_cn, w_pw2, b_pw2):
    h = x @ w_pw1 + b_pw1
    u = h[..., :D_MODEL] * jax.nn.sigmoid(h[..., D_MODEL:])
    ext = jnp.concatenate([past.astype(u.dtype), u], axis=1)
    conv = lax.conv_general_dilated(
        ext, w_dw[:, None, :].astype(ext.dtype), window_strides=(1,), padding='VALID',
        dimension_numbers=('NWC', 'WIO', 'NWC'), feature_group_count=D_MODEL) + b_dw
    z = layer_norm(conv, g_cn, b_cn)
    z = z * jax.nn.sigmoid(z)
    return z @ w_pw2 + b_pw2, ext[:, -CONV_STATE:]


def moe(x, r_w, r_b, w1, w3, w2):
    t = x.reshape(-1, D_MODEL)
    aff = jax.nn.sigmoid((t @ r_w).astype(jnp.float32))
    sel = aff + r_b.astype(jnp.float32)
    group_score = lax.top_k(sel.reshape(-1, N_GROUPS, EXPERTS_PER_GROUP), TOP_K)[0].sum(-1)
    g_star = jnp.argmax(group_score, axis=-1)
    in_group = (jnp.arange(N_EXPERTS) // EXPERTS_PER_GROUP)[None, :] == g_star[:, None]
    _, idx = lax.top_k(jnp.where(in_group, sel, -jnp.inf), TOP_K)
    w_sel = jnp.take_along_axis(aff, idx, axis=-1)
    w_sel = w_sel / jnp.sum(w_sel, axis=-1, keepdims=True)
    gates = jnp.einsum('tk,tke->te', w_sel, jax.nn.one_hot(idx, N_EXPERTS, dtype=jnp.float32))
    out = jnp.zeros(t.shape, jnp.float32)
    for e in range(N_EXPERTS):
        h = jax.nn.silu(t @ w1[e]) * (t @ w3[e])
        out = out + gates[:, e:e + 1] * (h @ w2[e]).astype(jnp.float32)
    return out.astype(x.dtype).reshape(x.shape)


def mla_shared_kv(h, pos, w_dkv, g_kv, w_kr):
    c_kv = rms_norm(h @ w_dkv, g_kv)
    k_pe = rope((h @ w_kr)[:, :, None, :], pos)[:, :, 0, :]
    return c_kv, k_pe


def mla_queries(x, pos, w_dq, g_q, w_uq, w_uk):
    c_q = rms_norm(x @ w_dq, g_q)
    q = jnp.einsum('bsr,rhd->bshd', c_q, w_uq)
    q_lat = jnp.einsum('bshd,rhd->bshr', q[..., :QK_NOPE], w_uk)
    q_pe = rope(q[..., QK_NOPE:], pos)
    return q_lat, q_pe


def attend_prompt(q_lat, q_pe, c_kv, k_pe):
    b, s = q_lat.shape[0], q_lat.shape[1]
    nb = s // Q_BLOCK
    ql = q_lat.reshape(b, nb, Q_BLOCK, N_HEADS, KV_LORA).transpose(1, 0, 2, 3, 4)
    qp = q_pe.reshape(b, nb, Q_BLOCK, N_HEADS, QK_ROPE).transpose(1, 0, 2, 3, 4)
    kpos = jnp.arange(s)

    def block(args):
        qlb, qpb, i = args
        sc = (jnp.einsum('bqhr,bkr->bhqk', qlb, c_kv)
              + jnp.einsum('bqhp,bkp->bhqk', qpb, k_pe)).astype(jnp.float32) * SCORE_SCALE
        qpos = i * Q_BLOCK + jnp.arange(Q_BLOCK)
        sc = jnp.where(kpos[None, :] <= qpos[:, None], sc, -jnp.inf)
        p = jax.nn.softmax(sc, axis=-1).astype(c_kv.dtype)
        return jnp.einsum('bhqk,bkr->bqhr', p, c_kv)

    o = lax.map(block, (ql, qp, jnp.arange(nb)))
    return o.transpose(1, 0, 2, 3, 4).reshape(b, s, N_HEADS, KV_LORA)


def attend_sample(q_lat, q_pe, c_past, kpe_past, c_new, kpe_new):
    t = q_lat.shape[1]
    n_past = c_past.shape[1]
    s_past = jnp.einsum('bqhr,bkr->bhqk', q_lat, c_past) + jnp.einsum('bqhp,bkp->bhqk', q_pe, kpe_past)
    s_new = jnp.einsum('bqhr,bkr->bhqk', q_lat, c_new) + jnp.einsum('bqhp,bkp->bhqk', q_pe, kpe_new)
    s_new = jnp.where(jnp.tril(jnp.ones((t, t), bool)), s_new.astype(jnp.float32), -jnp.inf)
    sc = jnp.concatenate([s_past.astype(jnp.float32), s_new], axis=-1) * SCORE_SCALE
    p = jax.nn.softmax(sc, axis=-1).astype(c_past.dtype)
    return (jnp.einsum('bhqk,bkr->bqhr', p[..., :n_past], c_past)
            + jnp.einsum('bhqk,bkr->bqhr', p[..., n_past:], c_new.astype(c_past.dtype)))


def trunk(x, pos, conv_past, kv_past, params):
    (a_w_pw1, a_b_pw1, a_w_dw, a_b_dw, a_g_cn, a_b_cn, a_w_pw2, a_b_pw2,
     ln_mix_g, ln_mix_b, ln_ffn_g, ln_ffn_b,
     b_w_dq, b_g_q, b_w_uq, b_w_o,
     s_w_dkv, s_g_kv, s_w_kr, s_w_uk, s_w_uv,
     r_w, r_b, e_w1, e_w3, e_w2) = params
    conv_new = []
    c_kv = None
    k_pe = None
    for l in range(DEPTH):
        if l < N_A_LAYERS:
            mix, buf = conv_module(x, conv_past[l], a_w_pw1[l], a_b_pw1[l], a_w_dw[l], a_b_dw[l],
                                   a_g_cn[l], a_b_cn[l], a_w_pw2[l], a_b_pw2[l])
            conv_new.append(buf)
        else:
            j = l - N_A_LAYERS
            if c_kv is None:
                c_kv, k_pe = mla_shared_kv(x, pos, s_w_dkv, s_g_kv, s_w_kr)
            q_lat, q_pe = mla_queries(x, pos, b_w_dq[j], b_g_q[j], b_w_uq[j], s_w_uk)
            if kv_past is None:
                o_lat = attend_prompt(q_lat, q_pe, c_kv, k_pe)
            else:
                o_lat = attend_sample(q_lat, q_pe, kv_past[0], kv_past[1], c_kv, k_pe)
            o = jnp.einsum('bshr,rhv->bshv', o_lat, s_w_uv)
            mix = o.reshape(o.shape[0], o.shape[1], N_HEADS * V_HEAD) @ b_w_o[j]
        x = layer_norm(ALPHA * x + mix, ln_mix_g[l], ln_mix_b[l])
        x = layer_norm(ALPHA * x + moe(x, r_w, r_b, e_w1[l], e_w3[l], e_w2[l]), ln_ffn_g[l], ln_ffn_b[l])
    return x, jnp.stack(conv_new), c_kv, k_pe


def setup_inputs(seed: int = 0) -> dict:
    key = jax.random.key(seed)
    ks = jax.random.split(key, 40)
    counter = iter(range(40))

    def nrm(shape, scale):
        return jax.random.normal(ks[next(counter)], shape, jnp.float32) * scale

    n_pages = PAST_LEN // PAGE_SIZE
    n_used = DEC_BATCH * n_pages
    n_pool = (n_used * 5) // 4
    d, f, e = D_MODEL, D_EXPERT, N_EXPERTS
    return {
        'x_prompt': nrm((BATCH, SEQ, d), 1.0),
        'x_sample': nrm((DEC_BATCH, DEC_SEQ, d), 1.0),
        'state_conv': nrm((N_A_LAYERS, DEC_BATCH, CONV_STATE, d), 0.5),
        'cache_ckv': nrm((n_pool, PAGE_SIZE, KV_LORA), 1.0),
        'cache_kpe': nrm((n_pool, PAGE_SIZE, QK_ROPE), 1.0),
        'page_table': jax.random.permutation(ks[next(counter)], n_pool)[:n_used].reshape(DEC_BATCH, n_pages).astype(jnp.int32),
        'a_w_pw1': nrm((N_A_LAYERS, d, 2 * d), d ** -0.5),
        'a_b_pw1': nrm((N_A_LAYERS, 2 * d), 0.02),
        'a_w_dw': nrm((N_A_LAYERS, CONV_WIDTH, d), CONV_WIDTH ** -0.5),
        'a_b_dw': nrm((N_A_LAYERS, d), 0.02),
        'a_g_cn': 1.0 + nrm((N_A_LAYERS, d), 0.02),
        'a_b_cn': nrm((N_A_LAYERS, d), 0.02),
        'a_w_pw2': nrm((N_A_LAYERS, d, d), BETA * d ** -0.5),
        'a_b_pw2': nrm((N_A_LAYERS, d), 0.02),
        'ln_mix_g': 1.0 + nrm((DEPTH, d), 0.02),
        'ln_mix_b': nrm((DEPTH, d), 0.02),
        'ln_ffn_g': 1.0 + nrm((DEPTH, d), 0.02),
        'ln_ffn_b': nrm((DEPTH, d), 0.02),
        'b_w_dq': nrm((N_B_LAYERS, d, Q_LORA), d ** -0.5),
        'b_g_q': 1.0 + nrm((N_B_LAYERS, Q_LORA), 0.02),
        'b_w_uq': nrm((N_B_LAYERS, Q_LORA, N_HEADS, QK_NOPE + QK_ROPE), Q_LORA ** -0.5),
        'b_w_o': nrm((N_B_LAYERS, N_HEADS * V_HEAD, d), BETA * (N_HEADS * V_HEAD) ** -0.5),
        's_w_dkv': nrm((d, KV_LORA), d ** -0.5),
        's_g_kv': 1.0 + nrm((KV_LORA,), 0.02),
        's_w_kr': nrm((d, QK_ROPE), d ** -0.5),
        's_w_uk': nrm((KV_LORA, N_HEADS, QK_NOPE), KV_LORA ** -0.5),
        's_w_uv': nrm((KV_LORA, N_HEADS, V_HEAD), BETA * KV_LORA ** -0.5),
        'r_w': nrm((d, e), d ** -0.5),
        'r_b': nrm((e,), 0.01),
        'e_w1': nrm((DEPTH, e, d, f), d ** -0.5),
        'e_w3': nrm((DEPTH, e, d, f), d ** -0.5),
        'e_w2': nrm((DEPTH, e, f, d), BETA * f ** -0.5),
    }


def reference(x_prompt, x_sample, state_conv, cache_ckv, cache_kpe, page_table,
              a_w_pw1, a_b_pw1, a_w_dw, a_b_dw, a_g_cn, a_b_cn, a_w_pw2, a_b_pw2,
              ln_mix_g, ln_mix_b, ln_ffn_g, ln_ffn_b,
              b_w_dq, b_g_q, b_w_uq, b_w_o,
              s_w_dkv, s_g_kv, s_w_kr, s_w_uk, s_w_uv,
              r_w, r_b, e_w1, e_w3, e_w2):
    params = (a_w_pw1, a_b_pw1, a_w_dw, a_b_dw, a_g_cn, a_b_cn, a_w_pw2, a_b_pw2,
              ln_mix_g, ln_mix_b, ln_ffn_g, ln_ffn_b,
              b_w_dq, b_g_q, b_w_uq, b_w_o,
              s_w_dkv, s_g_kv, s_w_kr, s_w_uk, s_w_uv,
              r_w, r_b, e_w1, e_w3, e_w2)
    b_p, s_p = x_prompt.shape[0], x_prompt.shape[1]
    conv0 = jnp.zeros((N_A_LAYERS, b_p, CONV_STATE, D_MODEL), x_prompt.dtype)
    y_prompt, conv_prompt, ckv_prompt, kpe_prompt = trunk(
        x_prompt, jnp.arange(s_p), conv0, None, params)
    b_s, t_s = x_sample.shape[0], x_sample.shape[1]
    c_past = cache_ckv[page_table].reshape(b_s, -1, KV_LORA)
    kpe_past = cache_kpe[page_table].reshape(b_s, -1, QK_ROPE)
    pos_s = c_past.shape[1] + jnp.arange(t_s)
    y_sample, conv_sample, ckv_sample, kpe_sample = trunk(
        x_sample, pos_s, state_conv, (c_past, kpe_past), params)
    return (y_prompt, y_sample, conv_prompt, conv_sample, ckv_prompt, kpe_prompt, ckv_sample, kpe_sample)
```

```python
import functools
import math

import jax
import jax.numpy as jnp
from jax import lax
from jax.experimental import pallas as pl
from jax.experimental.pallas import tpu as pltpu

D_MODEL = 1024
DEPTH = 4
N_A_LAYERS = DEPTH // 2
CONV_WIDTH = 31
CONV_STATE = CONV_WIDTH - 1
N_HEADS = 8
QK_NOPE = 128
QK_ROPE = 64
V_HEAD = 128
KV_LORA = 256
Q_LORA = 512
ROPE_BASE = 10000.0
SCORE_SCALE = (QK_NOPE + QK_ROPE) ** -0.5
N_EXPERTS = 16
N_GROUPS = 4
EXPERTS_PER_GROUP = N_EXPERTS // N_GROUPS
D_EXPERT = 512
ALPHA = (2 * DEPTH) ** 0.25
LN_EPS = 1e-5
RMS_EPS = 1e-6

LANES = 128
SUBLANES = 8
VMEM_LIMIT_BYTES = 56 * 1024 * 1024

ROPE_PAD = LANES
HIST_ROWS = 32
HIST_SKIP = HIST_ROWS - CONV_STATE

BF16 = jnp.bfloat16
F32 = jnp.float32
NEG_INF = float("-inf")


def _cparams(*sem):
    return pltpu.CompilerParams(dimension_semantics=sem, vmem_limit_bytes=VMEM_LIMIT_BYTES)


def _dot(a, b):
    return jnp.dot(a, b, preferred_element_type=F32)


def _dot_nt(a, b):
    return lax.dot_general(a, b, (((1,), (1,)), ((), ())), preferred_element_type=F32)


def _layer_norm(v, g, b):
    mu = jnp.mean(v, axis=-1, keepdims=True)
    d = v - mu
    var = jnp.mean(d * d, axis=-1, keepdims=True)
    return d * lax.rsqrt(var + LN_EPS) * g + b


def _rms_norm(v, g):
    return v * lax.rsqrt(jnp.mean(v * v, axis=-1, keepdims=True) + RMS_EPS) * g


def _sigmoid(v):
    return 1.0 / (1.0 + jnp.exp(-v))


def _full(shape):
    n = len(shape)
    return pl.BlockSpec(shape, lambda *_: (0,) * n)


def _conv_tail(conv, x, gcn, bcn, wpw2, bpw2, lg, lb):
    z = _layer_norm(conv, gcn, bcn)
    z = z * _sigmoid(z)
    mix = _dot(z.astype(BF16), wpw2) + bpw2
    return _layer_norm(ALPHA * x + mix, lg, lb)


def _conv_prompt_kernel(x_ref, wpw1_ref, bpw1_ref, w8_ref, bdw_ref, gcn_ref, bcn_ref, wpw2_ref, bpw2_ref,
                        lg_ref, lb_ref, y_ref, state_ref, ext_ref, conv_ref, *, ts, row_chunk, lane_chunk):
    s = pl.program_id(1)
    base = ext_ref.at[0]

    @pl.when(s == 0)
    def _():
        base[0:HIST_ROWS, :] = jnp.zeros((HIST_ROWS, D_MODEL), F32)

    @pl.when(s > 0)
    def _():
        base[0:HIST_ROWS, :] = base[ts:ts + HIST_ROWS, :]

    x = x_ref[...]
    h = _dot(x.astype(BF16), wpw1_ref[...]) + bpw1_ref[...]
    base[HIST_ROWS:HIST_ROWS + ts, :] = h[:, :D_MODEL] * _sigmoid(h[:, D_MODEL:])

    span = ts + HIST_ROWS - SUBLANES
    for j in range(1, SUBLANES):
        ext_ref[j, 0:span, :] = base[j:j + span, :]

    groups = row_chunk // SUBLANES

    def rows(r, carry):
        r0 = pl.multiple_of(r * row_chunk, row_chunk)
        for l0 in range(0, D_MODEL, lane_chunk):
            acc = jnp.broadcast_to(bdw_ref[:, l0:l0 + lane_chunk][None], (groups, SUBLANES, lane_chunk))
            for k in range(CONV_WIDTH):
                off = HIST_SKIP + k
                j, a = off % SUBLANES, off // SUBLANES
                start = pl.multiple_of(r0 + SUBLANES * a, SUBLANES)
                blk = ext_ref[j, pl.ds(start, row_chunk), l0:l0 + lane_chunk]
                acc = acc + blk.reshape(groups, SUBLANES, lane_chunk) * w8_ref[k, :, l0:l0 + lane_chunk][None]
            conv_ref[pl.ds(r0, row_chunk), l0:l0 + lane_chunk] = acc.reshape(row_chunk, lane_chunk)
        return carry

    lax.fori_loop(0, ts // row_chunk, rows, 0)

    y_ref[...] = _conv_tail(conv_ref[...], x, gcn_ref[...], bcn_ref[...], wpw2_ref[...], bpw2_ref[...],
                            lg_ref[...], lb_ref[...])

    @pl.when(s == pl.num_programs(1) - 1)
    def _():
        state_ref[...] = base[ts + HIST_SKIP:ts + HIST_ROWS, :]


def _conv_prompt(x, wpw1, bpw1, w8, bdw, gcn, bcn, wpw2, bpw2, lg, lb, *, ts=256):
    b, s, d = x.shape
    kern = functools.partial(_conv_prompt_kernel, ts=ts, row_chunk=32, lane_chunk=256)
    return pl.pallas_call(
        kern,
        grid=(b, s // ts),
        in_specs=[
            pl.BlockSpec((None, ts, d), lambda i, j: (i, j, 0)),
            _full(wpw1.shape), _full(bpw1.shape), _full(w8.shape), _full(bdw.shape), _full(gcn.shape),
            _full(bcn.shape), _full(wpw2.shape), _full(bpw2.shape), _full(lg.shape), _full(lb.shape),
        ],
        out_specs=[
            pl.BlockSpec((None, ts, d), lambda i, j: (i, j, 0)),
            pl.BlockSpec((None, CONV_STATE, d), lambda i, j: (i, 0, 0)),
        ],
        out_shape=[
            jax.ShapeDtypeStruct((b, s, d), F32),
            jax.ShapeDtypeStruct((b, CONV_STATE, d), F32),
        ],
        scratch_shapes=[
            pltpu.VMEM((SUBLANES, ts + HIST_ROWS, d), F32),
            pltpu.VMEM((ts, d), F32),
        ],
        compiler_params=_cparams("arbitrary", "arbitrary"),
        name="conv_prompt",
    )(x, wpw1, bpw1, w8, bdw, gcn, bcn, wpw2, bpw2, lg, lb)


def _conv_sample_kernel(x_ref, past_ref, wpw1_ref, bpw1_ref, w8_ref, bdw_ref, gcn_ref, bcn_ref, wpw2_ref,
                        bpw2_ref, lg_ref, lb_ref, y_ref, state_ref, ext_ref, conv_ref, *, bb, t, lane_chunk):
    x = x_ref[...].reshape(bb * t, D_MODEL)
    h = _dot(x.astype(BF16), wpw1_ref[...]) + bpw1_ref[...]
    u = h[:, :D_MODEL] * _sigmoid(h[:, D_MODEL:])
    ext_ref[:, HIST_SKIP:HIST_ROWS, :] = past_ref[...]
    ext_ref[:, HIST_ROWS:HIST_ROWS + t, :] = u.reshape(bb, t, D_MODEL)
    for l0 in range(0, D_MODEL, lane_chunk):
        acc = jnp.broadcast_to(bdw_ref[:, l0:l0 + lane_chunk][None], (bb, t, lane_chunk))
        for k in range(CONV_WIDTH):
            off = HIST_SKIP + k
            acc = acc + ext_ref[:, off:off + t, l0:l0 + lane_chunk] * w8_ref[k, :, l0:l0 + lane_chunk][None]
        conv_ref[:, :, l0:l0 + lane_chunk] = acc
    y = _conv_tail(conv_ref[...].reshape(bb * t, D_MODEL), x, gcn_ref[...], bcn_ref[...], wpw2_ref[...],
                   bpw2_ref[...], lg_ref[...], lb_ref[...])
    y_ref[...] = y.reshape(bb, t, D_MODEL)
    state_ref[...] = ext_ref[:, HIST_SKIP + t:HIST_ROWS + t, :]


def _conv_sample(x, past, wpw1, bpw1, w8, bdw, gcn, bcn, wpw2, bpw2, lg, lb, *, bb=16):
    b, t, d = x.shape
    assert t == SUBLANES
    kern = functools.partial(_conv_sample_kernel, bb=bb, t=t, lane_chunk=256)
    return pl.pallas_call(
        kern,
        grid=(b // bb,),
        in_specs=[
            pl.BlockSpec((bb, t, d), lambda i: (i, 0, 0)),
            pl.BlockSpec((bb, CONV_STATE, d), lambda i: (i, 0, 0)),
            _full(wpw1.shape), _full(bpw1.shape), _full(w8.shape), _full(bdw.shape), _full(gcn.shape),
            _full(bcn.shape), _full(wpw2.shape), _full(bpw2.shape), _full(lg.shape), _full(lb.shape),
        ],
        out_specs=[
            pl.BlockSpec((bb, t, d), lambda i: (i, 0, 0)),
            pl.BlockSpec((bb, CONV_STATE, d), lambda i: (i, 0, 0)),
        ],
        out_shape=[
            jax.ShapeDtypeStruct((b, t, d), F32),
            jax.ShapeDtypeStruct((b, CONV_STATE, d), F32),
        ],
        scratch_shapes=[
            pltpu.VMEM((bb, HIST_ROWS + t, d), F32),
            pltpu.VMEM((bb, t, d), F32),
        ],
        compiler_params=_cparams("arbitrary"),
        name="conv_sample",
    )(x, past, wpw1, bpw1, w8, bdw, gcn, bcn, wpw2, bpw2, lg, lb)


def _router_kernel(x_ref, rwt_ref, rb_ref, gates_ref):
    logits = lax.dot_general(rwt_ref[...], x_ref[...], (((1,), (1,)), ((), ())),
                             preferred_element_type=F32, precision=lax.Precision.HIGHEST)
    aff = _sigmoid(logits)
    sel = aff + rb_ref[...]
    g = N_GROUPS
    s = [sel[m * g:(m + 1) * g] for m in range(EXPERTS_PER_GROUP)]
    a = [aff[m * g:(m + 1) * g] for m in range(EXPERTS_PER_GROUP)]
    hi01, lo01 = jnp.maximum(s[0], s[1]), jnp.minimum(s[0], s[1])
    hi23, lo23 = jnp.maximum(s[2], s[3]), jnp.minimum(s[2], s[3])
    top1 = jnp.maximum(hi01, hi23)
    top2 = jnp.maximum(jnp.minimum(hi01, hi23), jnp.maximum(lo01, lo23))
    score = top1 + top2
    best = score[0:1]
    best_idx = jnp.zeros(best.shape, jnp.int32)
    for gi in range(1, g):
        better = score[gi:gi + 1] > best
        best = jnp.where(better, score[gi:gi + 1], best)
        best_idx = jnp.where(better, gi, best_idx)
    in_group = lax.broadcasted_iota(jnp.int32, score.shape, 0) == best_idx
    picked = []
    for m in range(EXPERTS_PER_GROUP):
        rank = jnp.zeros(score.shape, jnp.int32)
        for j in range(EXPERTS_PER_GROUP):
            if j == m:
                continue
            ahead = (s[j] >= s[m]) if j < m else (s[j] > s[m])
            rank = rank + ahead.astype(jnp.int32)
        picked.append(jnp.where(in_group & (rank < 2), a[m], 0.0))
    denom = picked[0] + picked[1] + picked[2] + picked[3]
    denom = jnp.sum(denom, axis=0, keepdims=True)
    inv = 1.0 / denom
    for m in range(EXPERTS_PER_GROUP):
        gates_ref[m * g:(m + 1) * g, :] = picked[m] * inv


def _router(x2d, rwt, rb, *, tm=1024):
    t, d = x2d.shape
    return pl.pallas_call(
        _router_kernel,
        grid=(t // tm,),
        in_specs=[pl.BlockSpec((tm, d), lambda i: (i, 0)), _full(rwt.shape), _full(rb.shape)],
        out_specs=pl.BlockSpec((N_EXPERTS, tm), lambda i: (0, i)),
        out_shape=jax.ShapeDtypeStruct((N_EXPERTS, t), F32),
        compiler_params=_cparams("arbitrary"),
        name="router",
    )(x2d, rwt, rb)


def _moe_kernel(x_ref, gates_ref, w1_ref, w3_ref, w2_ref, lg_ref, lb_ref, y_ref, xb_ref, acc_ref):
    e = pl.program_id(1)

    @pl.when(e == 0)
    def _():
        xb_ref[...] = x_ref[...].astype(BF16)
        acc_ref[...] = jnp.zeros(acc_ref.shape, F32)

    xb = xb_ref[...]
    h1 = _dot(xb, w1_ref[...])
    h3 = _dot(xb, w3_ref[...])
    h = (h1 * _sigmoid(h1) * h3).astype(BF16)
    lane = lax.broadcasted_iota(jnp.int32, gates_ref.shape, 1)
    gate = jnp.sum(jnp.where(lane == e, gates_ref[...], 0.0), axis=1, keepdims=True)
    acc_ref[...] += gate * _dot(h, w2_ref[...])

    @pl.when(e == pl.num_programs(1) - 1)
    def _():
        y_ref[...] = _layer_norm(ALPHA * x_ref[...] + acc_ref[...], lg_ref[...], lb_ref[...])


def _moe(x2d, gates, w1, w3, w2, lg, lb, *, tm=1024):
    t, d = x2d.shape
    e, _, f = w1.shape
    return pl.pallas_call(
        _moe_kernel,
        grid=(t // tm, e),
        in_specs=[
            pl.BlockSpec((tm, d), lambda i, j: (i, 0)),
            pl.BlockSpec((tm, e), lambda i, j: (i, 0)),
            pl.BlockSpec((None, d, f), lambda i, j: (j, 0, 0)),
            pl.BlockSpec((None, d, f), lambda i, j: (j, 0, 0)),
            pl.BlockSpec((None, f, d), lambda i, j: (j, 0, 0)),
            _full(lg.shape), _full(lb.shape),
        ],
        out_specs=pl.BlockSpec((tm, d), lambda i, j: (i, 0)),
        out_shape=jax.ShapeDtypeStruct((t, d), F32),
        scratch_shapes=[pltpu.VMEM((tm, d), BF16), pltpu.VMEM((tm, d), F32)],
        compiler_params=_cparams("arbitrary", "arbitrary"),
        name="moe",
    )(x2d, gates, w1, w3, w2, lg, lb)


def _shared_kv_kernel(x_ref, wdkv_ref, gkv_ref, wkr_ref, wkrs_ref, cos_ref, sin_ref,
                      ckv_ref, kpe_ref, ckvb_ref, kpeb_ref):
    xb = x_ref[...].astype(BF16)
    ckv = _rms_norm(_dot(xb, wdkv_ref[...]), gkv_ref[...])
    kpe = _dot(xb, wkr_ref[...]) * cos_ref[...] + _dot(xb, wkrs_ref[...]) * sin_ref[...]
    ckv_ref[...] = ckv
    kpe_ref[...] = kpe[:, :QK_ROPE]
    ckvb_ref[...] = ckv.astype(BF16)
    kpeb_ref[...] = kpe.astype(BF16)


def _shared_kv(x2d, wdkv, gkv, wkr, wkrs, cos, sin, *, tm, table_blocks):
    t, d = x2d.shape
    tab = pl.BlockSpec((tm, ROPE_PAD), lambda i: (i % table_blocks, 0))
    return pl.pallas_call(
        _shared_kv_kernel,
        grid=(t // tm,),
        in_specs=[pl.BlockSpec((tm, d), lambda i: (i, 0)), _full(wdkv.shape), _full(gkv.shape),
                  _full(wkr.shape), _full(wkrs.shape), tab, tab],
        out_specs=[
            pl.BlockSpec((tm, KV_LORA), lambda i: (i, 0)),
            pl.BlockSpec((tm, QK_ROPE), lambda i: (i, 0)),
            pl.BlockSpec((tm, KV_LORA), lambda i: (i, 0)),
            pl.BlockSpec((tm, ROPE_PAD), lambda i: (i, 0)),
        ],
        out_shape=[
            jax.ShapeDtypeStruct((t, KV_LORA), F32),
            jax.ShapeDtypeStruct((t, QK_ROPE), F32),
            jax.ShapeDtypeStruct((t, KV_LORA), BF16),
            jax.ShapeDtypeStruct((t, ROPE_PAD), BF16),
        ],
        compiler_params=_cparams("arbitrary"),
        name="shared_kv",
    )(x2d, wdkv, gkv, wkr, wkrs, cos, sin)


def _project_queries(x, cos, sin, wdq, gq, wuqn, wuqp, wuqps, wukt_ref, store):
    cq = _rms_norm(_dot(x.astype(BF16), wdq), gq).astype(BF16)
    qn = _dot(cq, wuqn)
    qp = _dot(cq, wuqp)
    qps = _dot(cq, wuqps)
    for h in range(N_HEADS):
        lat = _dot(qn[:, h * QK_NOPE:(h + 1) * QK_NOPE].astype(BF16), wukt_ref[h])
        sl = slice(h * ROPE_PAD, (h + 1) * ROPE_PAD)
        pe = qp[:, sl] * cos + qps[:, sl] * sin
        store(h, lat.astype(BF16), pe.astype(BF16))


def _project_output(o_heads, x, wuv_ref, wo, lg, lb):
    o = jnp.concatenate([_dot(o_heads[h].astype(BF16), wuv_ref[h]) for h in range(N_HEADS)], axis=-1)
    mix = _dot(o.astype(BF16), wo)
    return _layer_norm(ALPHA * x + mix, lg, lb)


def _attn_prompt_kernel(x_ref, ckv_ref, kpe_ref, cos_ref, sin_ref, wdq_ref, gq_ref, wuqn_ref, wuqp_ref,
                        wuqps_ref, wukt_ref, wuv_ref, wo_ref, lg_ref, lb_ref, y_ref,
                        ql_ref, qp_ref, m_ref, l_ref, acc_ref, *, tq, tk):
    i = pl.program_id(1)
    x = x_ref[...]

    def store(h, lat, pe):
        ql_ref[h * tq:(h + 1) * tq, :] = lat
        qp_ref[h * tq:(h + 1) * tq, :] = pe

    _project_queries(x, cos_ref[...], sin_ref[...], wdq_ref[...], gq_ref[...], wuqn_ref[...], wuqp_ref[...],
                     wuqps_ref[...], wukt_ref, store)

    rows = N_HEADS * tq
    m_ref[...] = jnp.full((rows, 1), NEG_INF, F32)
    l_ref[...] = jnp.zeros((rows, 1), F32)
    acc_ref[...] = jnp.zeros((rows, KV_LORA), F32)

    def block(kb, masked):
        k0 = pl.multiple_of(kb * tk, tk)
        kc = ckv_ref[pl.ds(k0, tk), :]
        kp = kpe_ref[pl.ds(k0, tk), :]
        s = (_dot_nt(ql_ref[...], kc) + _dot_nt(qp_ref[...], kp)) * SCORE_SCALE
        if masked:
            r = lax.broadcasted_iota(jnp.int32, (rows, tk), 0) & (tq - 1)
            c = lax.broadcasted_iota(jnp.int32, (rows, tk), 1)
            s = jnp.where(k0 + c <= i * tq + r, s, NEG_INF)
        m_old = m_ref[...]
        m_new = jnp.maximum(m_old, jnp.max(s, axis=-1, keepdims=True))
        p = jnp.exp(s - m_new)
        scale = jnp.exp(m_old - m_new)
        l_ref[...] = scale * l_ref[...] + jnp.sum(p, axis=-1, keepdims=True)
        acc_ref[...] = scale * acc_ref[...] + _dot(p.astype(BF16), kc)
        m_ref[...] = m_new

    n_full = (i * tq) // tk

    def body(kb, carry):
        block(kb, False)
        return carry

    lax.fori_loop(0, n_full, body, 0)
    block(n_full, True)

    inv = 1.0 / l_ref[...]
    o_heads = [acc_ref[h * tq:(h + 1) * tq, :] * inv[h * tq:(h + 1) * tq] for h in range(N_HEADS)]
    y_ref[...] = _project_output(o_heads, x, wuv_ref, wo_ref[...], lg_ref[...], lb_ref[...])


def _attn_prompt(x, ckvb, kpeb, cos, sin, wdq, gq, wuqn, wuqp, wuqps, wukt, wuv, wo, lg, lb, *, tq=256, tk=512):
    b, s, d = x.shape
    assert tk % tq == 0 and s % tk == 0
    kern = functools.partial(_attn_prompt_kernel, tq=tq, tk=tk)
    rows = N_HEADS * tq
    return pl.pallas_call(
        kern,
        grid=(b, s // tq),
        in_specs=[
            pl.BlockSpec((None, tq, d), lambda i, j: (i, j, 0)),
            pl.BlockSpec((None, s, KV_LORA), lambda i, j: (i, 0, 0)),
            pl.BlockSpec((None, s, ROPE_PAD), lambda i, j: (i, 0, 0)),
            pl.BlockSpec((tq, ROPE_PAD), lambda i, j: (j, 0)),
            pl.BlockSpec((tq, ROPE_PAD), lambda i, j: (j, 0)),
            _full(wdq.shape), _full(gq.shape), _full(wuqn.shape), _full(wuqp.shape), _full(wuqps.shape),
            _full(wukt.shape), _full(wuv.shape), _full(wo.shape), _full(lg.shape), _full(lb.shape),
        ],
        out_specs=pl.BlockSpec((None, tq, d), lambda i, j: (i, j, 0)),
        out_shape=jax.ShapeDtypeStruct((b, s, d), F32),
        scratch_shapes=[
            pltpu.VMEM((rows, KV_LORA), BF16),
            pltpu.VMEM((rows, ROPE_PAD), BF16),
            pltpu.VMEM((rows, 1), F32),
            pltpu.VMEM((rows, 1), F32),
            pltpu.VMEM((rows, KV_LORA), F32),
        ],
        compiler_params=_cparams("arbitrary", "arbitrary"),
        name="attn_prompt",
    )(x, ckvb, kpeb, cos, sin, wdq, gq, wuqn, wuqp, wuqps, wukt, wuv, wo, lg, lb)


def _q_sample_kernel(x_ref, cos_ref, sin_ref, wdq_ref, gq_ref, wuqn_ref, wuqp_ref, wuqps_ref, wukt_ref,
                     ql_ref, qp_ref, *, bb, t):
    x = x_ref[...].reshape(bb * t, D_MODEL)

    def store(h, lat, pe):
        ql_ref[:, h] = lat.reshape(bb, t, KV_LORA)
        qp_ref[:, h] = pe.reshape(bb, t, ROPE_PAD)

    _project_queries(x, cos_ref[...], sin_ref[...], wdq_ref[...], gq_ref[...], wuqn_ref[...], wuqp_ref[...],
                     wuqps_ref[...], wukt_ref, store)


def _q_sample(x, cos, sin, wdq, gq, wuqn, wuqp, wuqps, wukt, *, bb=32):
    b, t, d = x.shape
    kern = functools.partial(_q_sample_kernel, bb=bb, t=t)
    return pl.pallas_call(
        kern,
        grid=(b // bb,),
        in_specs=[
            pl.BlockSpec((bb, t, d), lambda i: (i, 0, 0)),
            pl.BlockSpec((bb * t, ROPE_PAD), lambda i: (i, 0)),
            pl.BlockSpec((bb * t, ROPE_PAD), lambda i: (i, 0)),
            _full(wdq.shape), _full(gq.shape), _full(wuqn.shape), _full(wuqp.shape), _full(wuqps.shape),
            _full(wukt.shape),
        ],
        out_specs=[
            pl.BlockSpec((bb, N_HEADS, t, KV_LORA), lambda i: (i, 0, 0, 0)),
            pl.BlockSpec((bb, N_HEADS, t, ROPE_PAD), lambda i: (i, 0, 0, 0)),
        ],
        out_shape=[
            jax.ShapeDtypeStruct((b, N_HEADS, t, KV_LORA), BF16),
            jax.ShapeDtypeStruct((b, N_HEADS, t, ROPE_PAD), BF16),
        ],
        compiler_params=_cparams("arbitrary"),
        name="q_sample",
    )(x, cos, sin, wdq, gq, wuqn, wuqp, wuqps, wukt)


def _attn_sample_kernel(pt_ref, ql_ref, qp_ref, cnew_ref, pnew_ref, *refs, pages, page, t):
    ckv_pages = refs[:pages]
    kpe_pages = refs[pages:2 * pages]
    o_ref, kc_ref, kp_ref, m_ref, l_ref, acc_ref = refs[2 * pages:]
    j = pl.program_id(1)
    rows = ql_ref.shape[0]

    @pl.when(j == 0)
    def _():
        m_ref[...] = jnp.full((rows, 1), NEG_INF, F32)
        l_ref[...] = jnp.zeros((rows, 1), F32)
        acc_ref[...] = jnp.zeros((rows, KV_LORA), F32)

    for p in range(pages):
        kc_ref[p * page:(p + 1) * page, :] = ckv_pages[p][...].astype(BF16)
        kp_ref[p * page:(p + 1) * page, :] = kpe_pages[p][...].astype(BF16)

    ql = ql_ref[...]
    qp = qp_ref[...]

    def update(s, v):
        m_old = m_ref[...]
        m_new = jnp.maximum(m_old, jnp.max(s, axis=-1, keepdims=True))
        p = jnp.exp(s - m_new)
        scale = jnp.exp(m_old - m_new)
        l_ref[...] = scale * l_ref[...] + jnp.sum(p, axis=-1, keepdims=True)
        acc_ref[...] = scale * acc_ref[...] + _dot(p.astype(BF16), v)
        m_ref[...] = m_new

    kc = kc_ref[...]
    update((_dot_nt(ql, kc) + _dot_nt(qp[:, :QK_ROPE], kp_ref[...])) * SCORE_SCALE, kc)

    @pl.when(j == pl.num_programs(1) - 1)
    def _():
        cn = cnew_ref[...]
        s = (_dot_nt(ql, cn) + _dot_nt(qp, pnew_ref[...])) * SCORE_SCALE
        n = cn.shape[0]
        qpos = lax.broadcasted_iota(jnp.int32, (rows, n), 0) & (t - 1)
        kpos = lax.broadcasted_iota(jnp.int32, (rows, n), 1)
        update(jnp.where(kpos <= qpos, s, NEG_INF), cn)
        o_ref[...] = acc_ref[...] * (1.0 / l_ref[...])


def _attn_sample(page_table, ql, qp, cnew, pnew, cache_ckv, cache_kpe, *, pages=16):
    b, rows, _ = ql.shape
    n_pages = page_table.shape[1]
    page = cache_ckv.shape[1]
    t = rows // N_HEADS
    n_new = cnew.shape[1]
    kern = functools.partial(_attn_sample_kernel, pages=pages, page=page, t=t)

    def page_spec(width, p):
        return pl.BlockSpec((None, page, width), lambda i, j, pt: (pt[i * n_pages + j * pages + p], 0, 0))

    grid_spec = pltpu.PrefetchScalarGridSpec(
        num_scalar_prefetch=1,
        grid=(b, n_pages // pages),
        in_specs=[
            pl.BlockSpec((None, rows, KV_LORA), lambda i, j, pt: (i, 0, 0)),
            pl.BlockSpec((None, rows, ROPE_PAD), lambda i, j, pt: (i, 0, 0)),
            pl.BlockSpec((None, n_new, KV_LORA), lambda i, j, pt: (i, 0, 0)),
            pl.BlockSpec((None, n_new, ROPE_PAD), lambda i, j, pt: (i, 0, 0)),
        ] + [page_spec(KV_LORA, p) for p in range(pages)] + [page_spec(QK_ROPE, p) for p in range(pages)],
        out_specs=pl.BlockSpec((None, rows, KV_LORA), lambda i, j, pt: (i, 0, 0)),
        scratch_shapes=[
            pltpu.VMEM((pages * page, KV_LORA), BF16),
            pltpu.VMEM((pages * page, QK_ROPE), BF16),
            pltpu.VMEM((rows, 1), F32),
            pltpu.VMEM((rows, 1), F32),
            pltpu.VMEM((rows, KV_LORA), F32),
        ],
    )
    return pl.pallas_call(
        kern,
        grid_spec=grid_spec,
        out_shape=jax.ShapeDtypeStruct((b, rows, KV_LORA), F32),
        compiler_params=_cparams("arbitrary", "arbitrary"),
        name="attn_sample",
    )(page_table.reshape(-1), ql, qp, cnew, pnew, *([cache_ckv] * pages), *([cache_kpe] * pages))


def _o_sample_kernel(o_ref, x_ref, wuv_ref, wo_ref, lg_ref, lb_ref, y_ref, *, bb, t):
    x = x_ref[...].reshape(bb * t, D_MODEL)
    o_heads = [o_ref[:, h].reshape(bb * t, KV_LORA) for h in range(N_HEADS)]
    y_ref[...] = _project_output(o_heads, x, wuv_ref, wo_ref[...], lg_ref[...], lb_ref[...]).reshape(bb, t, D_MODEL)


def _o_sample(o_lat, x, wuv, wo, lg, lb, *, bb=32):
    b, t, d = x.shape
    kern = functools.partial(_o_sample_kernel, bb=bb, t=t)
    return pl.pallas_call(
        kern,
        grid=(b // bb,),
        in_specs=[
            pl.BlockSpec((bb, N_HEADS, t, KV_LORA), lambda i: (i, 0, 0, 0)),
            pl.BlockSpec((bb, t, d), lambda i: (i, 0, 0)),
            _full(wuv.shape), _full(wo.shape), _full(lg.shape), _full(lb.shape),
        ],
        out_specs=pl.BlockSpec((bb, t, d), lambda i: (i, 0, 0)),
        out_shape=jax.ShapeDtypeStruct((b, t, d), F32),
        compiler_params=_cparams("arbitrary"),
        name="o_sample",
    )(o_lat, x, wuv, wo, lg, lb)


def _rope_tables(pos):
    half = QK_ROPE // 2
    inv_freq = ROPE_BASE ** (-jnp.arange(half, dtype=F32) / half)
    ang = pos.astype(F32)[:, None] * inv_freq[None, :]
    cos, sin = jnp.cos(ang), jnp.sin(ang)
    pad = jnp.zeros((pos.shape[0], ROPE_PAD - QK_ROPE), F32)
    return (jnp.concatenate([cos, cos, pad], axis=-1), jnp.concatenate([-sin, sin, pad], axis=-1))


def _swap_halves(w):
    half = QK_ROPE // 2
    return jnp.concatenate([w[..., half:], w[..., :half]], axis=-1)


def _pad_rope(w):
    return jnp.pad(w, [(0, 0)] * (w.ndim - 1) + [(0, ROPE_PAD - QK_ROPE)])


def _row(v):
    return v.reshape(1, -1)


def kernel(x_prompt, x_sample, state_conv, cache_ckv, cache_kpe, page_table, a_w_pw1, a_b_pw1, a_w_dw, a_b_dw, a_g_cn, a_b_cn, a_w_pw2, a_b_pw2, ln_mix_g, ln_mix_b, ln_ffn_g, ln_ffn_b, b_w_dq, b_g_q, b_w_uq, b_w_o, s_w_dkv, s_g_kv, s_w_kr, s_w_uk, s_w_uv, r_w, r_b, e_w1, e_w3, e_w2):
    bp, sp, d = x_prompt.shape
    bs, ts, _ = x_sample.shape
    past_len = page_table.shape[1] * cache_ckv.shape[1]

    a_w_pw1b, a_w_pw2b = a_w_pw1.astype(BF16), a_w_pw2.astype(BF16)
    w8 = jnp.broadcast_to(a_w_dw[:, :, None, :], (N_A_LAYERS, CONV_WIDTH, SUBLANES, d))
    e_w1b, e_w3b, e_w2b = e_w1.astype(BF16), e_w3.astype(BF16), e_w2.astype(BF16)
    perm = jnp.arange(N_EXPERTS).reshape(N_GROUPS, EXPERTS_PER_GROUP).T.reshape(-1)
    rwt = r_w.T[perm]
    rbp = r_b[perm].reshape(N_EXPERTS, 1)
    wdkv = s_w_dkv.astype(BF16)
    wkr = _pad_rope(s_w_kr).astype(BF16)
    wkrs = _pad_rope(_swap_halves(s_w_kr)).astype(BF16)
    wukt = jnp.transpose(s_w_uk, (1, 2, 0)).astype(BF16)
    wuv = jnp.transpose(s_w_uv, (1, 0, 2)).astype(BF16)
    wdq = b_w_dq.astype(BF16)
    uq_pe = b_w_uq[..., QK_NOPE:]
    wuqn = b_w_uq[..., :QK_NOPE].reshape(-1, Q_LORA, N_HEADS * QK_NOPE).astype(BF16)
    wuqp = _pad_rope(uq_pe).reshape(-1, Q_LORA, N_HEADS * ROPE_PAD).astype(BF16)
    wuqps = _pad_rope(_swap_halves(uq_pe)).reshape(-1, Q_LORA, N_HEADS * ROPE_PAD).astype(BF16)
    wo = b_w_o.astype(BF16)

    cos_p, sin_p = _rope_tables(jnp.arange(sp))
    cos_s1, sin_s1 = _rope_tables(past_len + jnp.arange(ts))
    cos_s, sin_s = jnp.tile(cos_s1, (bs, 1)), jnp.tile(sin_s1, (bs, 1))

    def moe_block(x, l):
        x2d = x.reshape(-1, d)
        gates_t = _router(x2d, rwt, rbp)
        gates = gates_t.reshape(EXPERTS_PER_GROUP, N_GROUPS, -1).transpose(2, 1, 0).reshape(-1, N_EXPERTS)
        y = _moe(x2d, gates, e_w1b[l], e_w3b[l], e_w2b[l], _row(ln_ffn_g[l]), _row(ln_ffn_b[l]))
        return y.reshape(x.shape)

    def conv_args(l):
        return (a_w_pw1b[l], _row(a_b_pw1[l]), w8[l], _row(a_b_dw[l]), _row(a_g_cn[l]), _row(a_b_cn[l]),
                a_w_pw2b[l], _row(a_b_pw2[l]), _row(ln_mix_g[l]), _row(ln_mix_b[l]))

    def q_args(j):
        return (wdq[j], _row(b_g_q[j]), wuqn[j], wuqp[j], wuqps[j], wukt)

    x = x_prompt
    conv_prompt = []
    for l in range(N_A_LAYERS):
        x, st = _conv_prompt(x, *conv_args(l))
        conv_prompt.append(st)
        x = moe_block(x, l)
    ckv_p, kpe_p, ckvb_p, kpeb_p = _shared_kv(x.reshape(-1, d), wdkv, _row(s_g_kv), wkr, wkrs, cos_p, sin_p,
                                              tm=512, table_blocks=sp // 512)
    ckvb_p3, kpeb_p3 = ckvb_p.reshape(bp, sp, KV_LORA), kpeb_p.reshape(bp, sp, ROPE_PAD)
    for l in range(N_A_LAYERS, DEPTH):
        j = l - N_A_LAYERS
        x = _attn_prompt(x, ckvb_p3, kpeb_p3, cos_p, sin_p, *q_args(j), wuv, wo[j],
                         _row(ln_mix_g[l]), _row(ln_mix_b[l]))
        x = moe_block(x, l)
    y_prompt = x

    x = x_sample
    conv_sample = []
    for l in range(N_A_LAYERS):
        x, st = _conv_sample(x, state_conv[l], *conv_args(l))
        conv_sample.append(st)
        x = moe_block(x, l)
    ckv_s, kpe_s, ckvb_s, kpeb_s = _shared_kv(x.reshape(-1, d), wdkv, _row(s_g_kv), wkr, wkrs, cos_s, sin_s,
                                              tm=512, table_blocks=(bs * ts) // 512)
    new_rows = LANES
    cnew = jnp.pad(ckvb_s.reshape(bs, ts, KV_LORA), ((0, 0), (0, new_rows - ts), (0, 0)))
    pnew = jnp.pad(kpeb_s.reshape(bs, ts, ROPE_PAD), ((0, 0), (0, new_rows - ts), (0, 0)))
    for l in range(N_A_LAYERS, DEPTH):
        j = l - N_A_LAYERS
        ql, qp = _q_sample(x, cos_s, sin_s, *q_args(j))
        o_lat = _attn_sample(page_table, ql.reshape(bs, N_HEADS * ts, KV_LORA),
                             qp.reshape(bs, N_HEADS * ts, ROPE_PAD), cnew, pnew, cache_ckv, cache_kpe)
        x = _o_sample(o_lat.reshape(bs, N_HEADS, ts, KV_LORA), x, wuv, wo[j],
                      _row(ln_mix_g[l]), _row(ln_mix_b[l]))
        x = moe_block(x, l)
    y_sample = x

    return (y_prompt, y_sample, jnp.stack(conv_prompt), jnp.stack(conv_sample),
            ckv_p.reshape(bp, sp, KV_LORA), kpe_p.reshape(bp, sp, QK_ROPE),
            ckv_s.reshape(bs, ts, KV_LORA), kpe_s.reshape(bs, ts, QK_ROPE))
```

```python
import functools

import jax
import jax.numpy as jnp
from jax import lax
from jax.experimental import pallas as pl
from jax.experimental.pallas import tpu as pltpu

D_MODEL = 1024
DEPTH = 4
N_A_LAYERS = DEPTH // 2
CONV_WIDTH = 31
CONV_STATE = CONV_WIDTH - 1
N_HEADS = 8
QK_NOPE = 128
QK_ROPE = 64
V_HEAD = 128
KV_LORA = 256
Q_LORA = 512
ROPE_BASE = 10000.0
SCORE_SCALE = (QK_NOPE + QK_ROPE) ** -0.5
N_EXPERTS = 16
N_GROUPS = 4
EXPERTS_PER_GROUP = N_EXPERTS // N_GROUPS
D_EXPERT = 512
ALPHA = (2 * DEPTH) ** 0.25
LN_EPS = 1e-5
RMS_EPS = 1e-6

LANES = 128
SUBLANES = 8
VMEM_LIMIT_BYTES = 56 * 1024 * 1024

ROPE_PAD = LANES
HIST_ROWS = 32
HIST_SKIP = HIST_ROWS - CONV_STATE

MEMBER_PAIRS = ((0, 1), (0, 2), (0, 3), (1, 2), (1, 3), (2, 3))
N_CLASSES = N_GROUPS * len(MEMBER_PAIRS)
CLASS_ROWS = 32
GATE_PAD = LANES
MOE_TILE = 512

BF16 = jnp.bfloat16
F32 = jnp.float32
NEG_INF = float("-inf")


def _cparams(*sem):
    return pltpu.CompilerParams(dimension_semantics=sem, vmem_limit_bytes=VMEM_LIMIT_BYTES)


def _dot(a, b):
    return jnp.dot(a, b, preferred_element_type=F32)


def _dot_nt(a, b):
    return lax.dot_general(a, b, (((1,), (1,)), ((), ())), preferred_element_type=F32)


def _layer_norm(v, g, b):
    mu = jnp.mean(v, axis=-1, keepdims=True)
    d = v - mu
    var = jnp.mean(d * d, axis=-1, keepdims=True)
    return d * lax.rsqrt(var + LN_EPS) * g + b


def _rms_norm(v, g):
    return v * lax.rsqrt(jnp.mean(v * v, axis=-1, keepdims=True) + RMS_EPS) * g


def _sigmoid(v):
    return 1.0 / (1.0 + jnp.exp(-v))


def _full(shape):
    n = len(shape)
    return pl.BlockSpec(shape, lambda *_: (0,) * n)


def _conv_tail(conv, x, gcn, bcn, wpw2, bpw2, lg, lb):
    z = _layer_norm(conv, gcn, bcn)
    z = z * _sigmoid(z)
    mix = _dot(z.astype(BF16), wpw2) + bpw2
    return _layer_norm(ALPHA * x + mix, lg, lb)


def _conv_prompt_kernel(x_ref, wpw1_ref, bpw1_ref, w8_ref, bdw_ref, gcn_ref, bcn_ref, wpw2_ref, bpw2_ref,
                        lg_ref, lb_ref, y_ref, state_ref, ext_ref, conv_ref, *, ts, row_chunk, lane_chunk):
    s = pl.program_id(1)
    base = ext_ref.at[0]

    @pl.when(s == 0)
    def _():
        base[0:HIST_ROWS, :] = jnp.zeros((HIST_ROWS, D_MODEL), F32)

    @pl.when(s > 0)
    def _():
        base[0:HIST_ROWS, :] = base[ts:ts + HIST_ROWS, :]

    x = x_ref[...]
    h = _dot(x.astype(BF16), wpw1_ref[...]) + bpw1_ref[...]
    base[HIST_ROWS:HIST_ROWS + ts, :] = h[:, :D_MODEL] * _sigmoid(h[:, D_MODEL:])

    span = ts + HIST_ROWS - SUBLANES
    for j in range(1, SUBLANES):
        ext_ref[j, 0:span, :] = base[j:j + span, :]

    groups = row_chunk // SUBLANES

    def rows(r, carry):
        r0 = pl.multiple_of(r * row_chunk, row_chunk)
        for l0 in range(0, D_MODEL, lane_chunk):
            acc = jnp.broadcast_to(bdw_ref[:, l0:l0 + lane_chunk][None], (groups, SUBLANES, lane_chunk))
            for k in range(CONV_WIDTH):
                off = HIST_SKIP + k
                j, a = off % SUBLANES, off // SUBLANES
                start = pl.multiple_of(r0 + SUBLANES * a, SUBLANES)
                blk = ext_ref[j, pl.ds(start, row_chunk), l0:l0 + lane_chunk]
                acc = acc + blk.reshape(groups, SUBLANES, lane_chunk) * w8_ref[k, :, l0:l0 + lane_chunk][None]
            conv_ref[pl.ds(r0, row_chunk), l0:l0 + lane_chunk] = acc.reshape(row_chunk, lane_chunk)
        return carry

    lax.fori_loop(0, ts // row_chunk, rows, 0)

    y_ref[...] = _conv_tail(conv_ref[...], x, gcn_ref[...], bcn_ref[...], wpw2_ref[...], bpw2_ref[...],
                            lg_ref[...], lb_ref[...])

    @pl.when(s == pl.num_programs(1) - 1)
    def _():
        state_ref[...] = base[ts + HIST_SKIP:ts + HIST_ROWS, :]


def _conv_prompt(x, wpw1, bpw1, w8, bdw, gcn, bcn, wpw2, bpw2, lg, lb, *, ts=256):
    b, s, d = x.shape
    kern = functools.partial(_conv_prompt_kernel, ts=ts, row_chunk=32, lane_chunk=256)
    return pl.pallas_call(
        kern,
        grid=(b, s // ts),
        in_specs=[
            pl.BlockSpec((None, ts, d), lambda i, j: (i, j, 0)),
            _full(wpw1.shape), _full(bpw1.shape), _full(w8.shape), _full(bdw.shape), _full(gcn.shape),
            _full(bcn.shape), _full(wpw2.shape), _full(bpw2.shape), _full(lg.shape), _full(lb.shape),
        ],
        out_specs=[
            pl.BlockSpec((None, ts, d), lambda i, j: (i, j, 0)),
            pl.BlockSpec((None, CONV_STATE, d), lambda i, j: (i, 0, 0)),
        ],
        out_shape=[
            jax.ShapeDtypeStruct((b, s, d), F32),
            jax.ShapeDtypeStruct((b, CONV_STATE, d), F32),
        ],
        scratch_shapes=[
            pltpu.VMEM((SUBLANES, ts + HIST_ROWS, d), F32),
            pltpu.VMEM((ts, d), F32),
        ],
        compiler_params=_cparams("arbitrary", "arbitrary"),
        name="conv_prompt",
    )(x, wpw1, bpw1, w8, bdw, gcn, bcn, wpw2, bpw2, lg, lb)


def _conv_sample_kernel(x_ref, past_ref, wpw1_ref, bpw1_ref, w8_ref, bdw_ref, gcn_ref, bcn_ref, wpw2_ref,
                        bpw2_ref, lg_ref, lb_ref, y_ref, state_ref, ext_ref, conv_ref, *, bb, t, lane_chunk):
    x = x_ref[...].reshape(bb * t, D_MODEL)
    h = _dot(x.astype(BF16), wpw1_ref[...]) + bpw1_ref[...]
    u = h[:, :D_MODEL] * _sigmoid(h[:, D_MODEL:])
    ext_ref[:, HIST_SKIP:HIST_ROWS, :] = past_ref[...]
    ext_ref[:, HIST_ROWS:HIST_ROWS + t, :] = u.reshape(bb, t, D_MODEL)
    for l0 in range(0, D_MODEL, lane_chunk):
        acc = jnp.broadcast_to(bdw_ref[:, l0:l0 + lane_chunk][None], (bb, t, lane_chunk))
        for k in range(CONV_WIDTH):
            off = HIST_SKIP + k
            acc = acc + ext_ref[:, off:off + t, l0:l0 + lane_chunk] * w8_ref[k, :, l0:l0 + lane_chunk][None]
        conv_ref[:, :, l0:l0 + lane_chunk] = acc
    y = _conv_tail(conv_ref[...].reshape(bb * t, D_MODEL), x, gcn_ref[...], bcn_ref[...], wpw2_ref[...],
                   bpw2_ref[...], lg_ref[...], lb_ref[...])
    y_ref[...] = y.reshape(bb, t, D_MODEL)
    state_ref[...] = ext_ref[:, HIST_SKIP + t:HIST_ROWS + t, :]


def _conv_sample(x, past, wpw1, bpw1, w8, bdw, gcn, bcn, wpw2, bpw2, lg, lb, *, bb=16):
    b, t, d = x.shape
    assert t == SUBLANES
    kern = functools.partial(_conv_sample_kernel, bb=bb, t=t, lane_chunk=256)
    return pl.pallas_call(
        kern,
        grid=(b // bb,),
        in_specs=[
            pl.BlockSpec((bb, t, d), lambda i: (i, 0, 0)),
            pl.BlockSpec((bb, CONV_STATE, d), lambda i: (i, 0, 0)),
            _full(wpw1.shape), _full(bpw1.shape), _full(w8.shape), _full(bdw.shape), _full(gcn.shape),
            _full(bcn.shape), _full(wpw2.shape), _full(bpw2.shape), _full(lg.shape), _full(lb.shape),
        ],
        out_specs=[
            pl.BlockSpec((bb, t, d), lambda i: (i, 0, 0)),
            pl.BlockSpec((bb, CONV_STATE, d), lambda i: (i, 0, 0)),
        ],
        out_shape=[
            jax.ShapeDtypeStruct((b, t, d), F32),
            jax.ShapeDtypeStruct((b, CONV_STATE, d), F32),
        ],
        scratch_shapes=[
            pltpu.VMEM((bb, HIST_ROWS + t, d), F32),
            pltpu.VMEM((bb, t, d), F32),
        ],
        compiler_params=_cparams("arbitrary"),
        name="conv_sample",
    )(x, past, wpw1, bpw1, w8, bdw, gcn, bcn, wpw2, bpw2, lg, lb)


def _router_kernel(x_ref, rwt_ref, rb_ref, gates_ref, cls_ref, xaug_ref, gcol_ref):
    logits = lax.dot_general(rwt_ref[...], x_ref[...], (((1,), (1,)), ((), ())),
                             preferred_element_type=F32, precision=lax.Precision.HIGHEST)
    aff = _sigmoid(logits)
    sel = aff + rb_ref[...]
    g = N_GROUPS
    s = [sel[m * g:(m + 1) * g] for m in range(EXPERTS_PER_GROUP)]
    a = [aff[m * g:(m + 1) * g] for m in range(EXPERTS_PER_GROUP)]
    hi01, lo01 = jnp.maximum(s[0], s[1]), jnp.minimum(s[0], s[1])
    hi23, lo23 = jnp.maximum(s[2], s[3]), jnp.minimum(s[2], s[3])
    top1 = jnp.maximum(hi01, hi23)
    top2 = jnp.maximum(jnp.minimum(hi01, hi23), jnp.maximum(lo01, lo23))
    score = top1 + top2
    best = score[0:1]
    best_idx = jnp.zeros(best.shape, jnp.int32)
    for gi in range(1, g):
        better = score[gi:gi + 1] > best
        best = jnp.where(better, score[gi:gi + 1], best)
        best_idx = jnp.where(better, gi, best_idx)
    in_group = lax.broadcasted_iota(jnp.int32, score.shape, 0) == best_idx
    picked, chosen = [], []
    for m in range(EXPERTS_PER_GROUP):
        rank = jnp.zeros(score.shape, jnp.int32)
        for j in range(EXPERTS_PER_GROUP):
            if j == m:
                continue
            ahead = (s[j] >= s[m]) if j < m else (s[j] > s[m])
            rank = rank + ahead.astype(jnp.int32)
        keep = in_group & (rank < 2)
        picked.append(jnp.where(keep, a[m], 0.0))
        chosen.append(jnp.where(keep, 1.0, 0.0))
    p = [jnp.sum(v, axis=0, keepdims=True) for v in picked]
    on = [jnp.sum(v, axis=0, keepdims=True) > 0.5 for v in chosen]
    inv = 1.0 / (p[0] + p[1] + p[2] + p[3])
    for m in range(EXPERTS_PER_GROUP):
        gates_ref[m * g:(m + 1) * g, :] = picked[m] * inv
    lo = jnp.where(on[0], 0, jnp.where(on[1], 1, 2))
    hi = jnp.where(on[3], 3, jnp.where(on[2], 2, 1))
    p_lo = jnp.where(on[0], p[0], jnp.where(on[1], p[1], p[2]))
    p_hi = jnp.where(on[3], p[3], jnp.where(on[2], p[2], p[1]))
    pair = jnp.where(lo == 0, hi - 1, jnp.where(lo == 1, hi + 1, len(MEMBER_PAIRS) - 1))
    cls_ref[...] = best_idx * len(MEMBER_PAIRS) + pair
    gcol_ref[...] = jnp.zeros(gcol_ref.shape, F32)
    gcol_ref[0:1, :] = p_lo * inv
    gcol_ref[1:2, :] = p_hi * inv
    xaug_ref[:, :D_MODEL] = x_ref[...]
    xaug_ref[:, D_MODEL:] = gcol_ref[...].T


def _router(x2d, rwt, rb, *, tm=1024):
    t, d = x2d.shape
    return pl.pallas_call(
        _router_kernel,
        grid=(t // tm,),
        in_specs=[pl.BlockSpec((tm, d), lambda i: (i, 0)), _full(rwt.shape), _full(rb.shape)],
        out_specs=[
            pl.BlockSpec((N_EXPERTS, tm), lambda i: (0, i)),
            pl.BlockSpec((1, tm), lambda i: (0, i)),
            pl.BlockSpec((tm, d + GATE_PAD), lambda i: (i, 0)),
        ],
        out_shape=[
            jax.ShapeDtypeStruct((N_EXPERTS, t), F32),
            jax.ShapeDtypeStruct((1, t), jnp.int32),
            jax.ShapeDtypeStruct((t, d + GATE_PAD), F32),
        ],
        scratch_shapes=[pltpu.VMEM((GATE_PAD, tm), F32)],
        compiler_params=_cparams("arbitrary"),
        name="router",
    )(x2d, rwt, rb)


def _positions_kernel(cls_ref, pos_ref, tcls_ref, nt_ref, cnt_ref, start_ref, run_ref, *, tb, tm):
    ph = pl.program_id(0)
    i = pl.program_id(1)
    shift = tm.bit_length() - 1
    onehot = lax.broadcasted_iota(jnp.int32, (CLASS_ROWS, tb), 0) == cls_ref[...]
    ohf = jnp.where(onehot, 1.0, 0.0)

    @pl.when((ph == 0) & (i == 0))
    def _():
        cnt_ref[...] = jnp.zeros(cnt_ref.shape, F32)

    @pl.when(ph == 0)
    def _():
        cnt_ref[...] += jnp.sum(ohf, axis=1, keepdims=True)

    @pl.when((ph == 1) & (i == 0))
    def _():
        cnt = cnt_ref[...].astype(jnp.int32)
        padded = (((cnt + (tm - 1)) >> shift) << shift).astype(F32)
        r = lax.broadcasted_iota(jnp.int32, (CLASS_ROWS, CLASS_ROWS), 0)
        c = lax.broadcasted_iota(jnp.int32, (CLASS_ROWS, CLASS_ROWS), 1)
        start = jnp.dot(jnp.where(c < r, 1.0, 0.0), padded, preferred_element_type=F32,
                        precision=lax.Precision.HIGHEST)
        start_ref[...] = start
        run_ref[...] = jnp.zeros(run_ref.shape, F32)
        tile_start = (lax.broadcasted_iota(jnp.int32, (CLASS_ROWS, LANES), 1) << shift).astype(F32)
        real = lax.broadcasted_iota(jnp.int32, (CLASS_ROWS, LANES), 0) < N_CLASSES
        below = jnp.where(real, jnp.where(start <= tile_start, 1.0, 0.0), 0.0)
        tcls_ref[...] = jnp.sum(below, axis=0, keepdims=True).astype(jnp.int32) - 1
        nt_ref[...] = jnp.sum(padded, axis=0, keepdims=True).astype(jnp.int32) >> shift

    @pl.when(ph == 1)
    def _():
        rr = lax.broadcasted_iota(jnp.int32, (tb, tb), 0)
        cc = lax.broadcasted_iota(jnp.int32, (tb, tb), 1)
        upper = jnp.where(rr <= cc, 1.0, 0.0).astype(BF16)
        cum = _dot(ohf.astype(BF16), upper)
        base = run_ref[:, 0:1] + start_ref[:, 0:1] - 1.0
        pos_ref[...] = jnp.sum(ohf * (cum + base), axis=0, keepdims=True).astype(jnp.int32)
        run_ref[...] += cum[:, tb - 1:tb]


def _positions(cls, *, tb=512, tm=MOE_TILE):
    t = cls.shape[1]
    assert tm & (tm - 1) == 0 and (t + N_CLASSES * tm) // tm <= LANES
    kern = functools.partial(_positions_kernel, tb=tb, tm=tm)
    return pl.pallas_call(
        kern,
        grid=(2, t // tb),
        in_specs=[pl.BlockSpec((1, tb), lambda p, i: (0, i))],
        out_specs=[
            pl.BlockSpec((1, tb), lambda p, i: (0, i * p)),
            pl.BlockSpec((1, LANES), lambda p, i: (0, 0)),
            pl.BlockSpec((1, LANES), lambda p, i: (0, 0)),
        ],
        out_shape=[
            jax.ShapeDtypeStruct((1, t), jnp.int32),
            jax.ShapeDtypeStruct((1, LANES), jnp.int32),
            jax.ShapeDtypeStruct((1, LANES), jnp.int32),
        ],
        scratch_shapes=[pltpu.VMEM((CLASS_ROWS, LANES), F32)] * 3,
        compiler_params=_cparams("arbitrary", "arbitrary"),
        name="positions",
    )(cls)


def _row_copy(src_hbm, dst_hbm, src_row, dst_row, n, sem):
    return pltpu.make_async_copy(src_hbm.at[pl.ds(src_row, n), :], dst_hbm.at[pl.ds(dst_row, n), :], sem)


def _row_scatter_kernel(pos_ref, src_hbm, init_hbm, dst_hbm, sem, *, tb, unroll):
    del init_hbm
    base = pl.program_id(0) * tb

    def issue(c, carry):
        for u in range(unroll):
            r = c * unroll + u
            _row_copy(src_hbm, dst_hbm, base + r, pos_ref[0, r], 1, sem).start(priority=u % 2)
        return carry

    lax.fori_loop(0, tb // unroll, issue, 0)
    _row_copy(src_hbm, dst_hbm, 0, 0, tb, sem).wait()


def _row_scatter(pos3, src, init, *, unroll=8):
    nb, _, tb = pos3.shape
    kern = functools.partial(_row_scatter_kernel, tb=tb, unroll=unroll)
    return pl.pallas_call(
        kern,
        grid=(nb,),
        in_specs=[
            pl.BlockSpec((None, 1, tb), lambda i: (i, 0, 0), memory_space=pltpu.SMEM),
            pl.BlockSpec(memory_space=pl.ANY),
            pl.BlockSpec(memory_space=pl.ANY),
        ],
        out_specs=pl.BlockSpec(memory_space=pl.ANY),
        out_shape=jax.ShapeDtypeStruct(init.shape, init.dtype),
        scratch_shapes=[pltpu.SemaphoreType.DMA],
        input_output_aliases={2: 0},
        compiler_params=_cparams("arbitrary"),
        name="row_scatter",
    )(pos3, src, init)


def _row_gather_kernel(pos_ref, src_hbm, dst_hbm, sem, *, tb, unroll):
    base = pl.program_id(0) * tb

    def issue(c, carry):
        for u in range(unroll):
            r = c * unroll + u
            _row_copy(src_hbm, dst_hbm, pos_ref[0, r], base + r, 1, sem).start(priority=u % 2)
        return carry

    lax.fori_loop(0, tb // unroll, issue, 0)
    _row_copy(src_hbm, dst_hbm, 0, 0, tb, sem).wait()


def _row_gather(pos3, src, *, unroll=8):
    nb, _, tb = pos3.shape
    kern = functools.partial(_row_gather_kernel, tb=tb, unroll=unroll)
    return pl.pallas_call(
        kern,
        grid=(nb,),
        in_specs=[
            pl.BlockSpec((None, 1, tb), lambda i: (i, 0, 0), memory_space=pltpu.SMEM),
            pl.BlockSpec(memory_space=pl.ANY),
        ],
        out_specs=pl.BlockSpec(memory_space=pl.ANY),
        out_shape=jax.ShapeDtypeStruct((nb * tb, src.shape[1]), src.dtype),
        scratch_shapes=[pltpu.SemaphoreType.DMA],
        compiler_params=_cparams("arbitrary"),
        name="row_gather",
    )(pos3, src)


def _moe_sorted_kernel(ea_ref, eb_ref, nt_ref, xs_ref, w1a_ref, w3a_ref, w2a_ref, w1b_ref, w3b_ref, w2b_ref,
                       lg_ref, lb_ref, y_ref):
    del ea_ref, eb_ref
    used = pl.program_id(0) < nt_ref[0]

    @pl.when(jnp.logical_not(used))
    def _():
        y_ref[...] = jnp.zeros(y_ref.shape, F32)

    @pl.when(used)
    def _():
        x = xs_ref[:, :D_MODEL]
        xb = x.astype(BF16)

        def ffn(w1_ref, w3_ref, w2_ref):
            h1 = _dot(xb, w1_ref[...])
            h3 = _dot(xb, w3_ref[...])
            return _dot((h1 * _sigmoid(h1) * h3).astype(BF16), w2_ref[...])

        out = xs_ref[:, D_MODEL:D_MODEL + 1] * ffn(w1a_ref, w3a_ref, w2a_ref)
        out = out + xs_ref[:, D_MODEL + 1:D_MODEL + 2] * ffn(w1b_ref, w3b_ref, w2b_ref)
        y_ref[...] = _layer_norm(ALPHA * x + out, lg_ref[...], lb_ref[...])


def _moe_sorted(ea, eb, nt, xs, w1, w3, w2, lg, lb, *, tm=MOE_TILE):
    p, da = xs.shape
    _, d, f = w1.shape

    def row_block(i, ea, eb, nt):
        return (i, 0)

    def wa(i, ea, eb, nt):
        return (ea[i], 0, 0)

    def wb(i, ea, eb, nt):
        return (eb[i], 0, 0)

    grid_spec = pltpu.PrefetchScalarGridSpec(
        num_scalar_prefetch=3,
        grid=(p // tm,),
        in_specs=[
            pl.BlockSpec((tm, da), row_block),
            pl.BlockSpec((None, d, f), wa), pl.BlockSpec((None, d, f), wa), pl.BlockSpec((None, f, d), wa),
            pl.BlockSpec((None, d, f), wb), pl.BlockSpec((None, d, f), wb), pl.BlockSpec((None, f, d), wb),
            pl.BlockSpec(lg.shape, lambda i, ea, eb, nt: (0, 0)),
            pl.BlockSpec(lb.shape, lambda i, ea, eb, nt: (0, 0)),
        ],
        out_specs=pl.BlockSpec((tm, d), row_block),
    )
    return pl.pallas_call(
        _moe_sorted_kernel,
        grid_spec=grid_spec,
        out_shape=jax.ShapeDtypeStruct((p, d), F32),
        compiler_params=_cparams("arbitrary"),
        name="moe_sorted",
    )(ea, eb, nt, xs, w1, w3, w2, w1, w3, w2, lg, lb)


def _moe_kernel(x_ref, gates_ref, w1_ref, w3_ref, w2_ref, lg_ref, lb_ref, y_ref, xb_ref, acc_ref):
    e = pl.program_id(1)

    @pl.when(e == 0)
    def _():
        xb_ref[...] = x_ref[...].astype(BF16)
        acc_ref[...] = jnp.zeros(acc_ref.shape, F32)

    xb = xb_ref[...]
    h1 = _dot(xb, w1_ref[...])
    h3 = _dot(xb, w3_ref[...])
    h = (h1 * _sigmoid(h1) * h3).astype(BF16)
    lane = lax.broadcasted_iota(jnp.int32, gates_ref.shape, 1)
    gate = jnp.sum(jnp.where(lane == e, gates_ref[...], 0.0), axis=1, keepdims=True)
    acc_ref[...] += gate * _dot(h, w2_ref[...])

    @pl.when(e == pl.num_programs(1) - 1)
    def _():
        y_ref[...] = _layer_norm(ALPHA * x_ref[...] + acc_ref[...], lg_ref[...], lb_ref[...])


def _moe(x2d, gates, w1, w3, w2, lg, lb, *, tm=1024):
    t, d = x2d.shape
    e, _, f = w1.shape
    return pl.pallas_call(
        _moe_kernel,
        grid=(t // tm, e),
        in_specs=[
            pl.BlockSpec((tm, d), lambda i, j: (i, 0)),
            pl.BlockSpec((tm, e), lambda i, j: (i, 0)),
            pl.BlockSpec((None, d, f), lambda i, j: (j, 0, 0)),
            pl.BlockSpec((None, d, f), lambda i, j: (j, 0, 0)),
            pl.BlockSpec((None, f, d), lambda i, j: (j, 0, 0)),
            _full(lg.shape), _full(lb.shape),
        ],
        out_specs=pl.BlockSpec((tm, d), lambda i, j: (i, 0)),
        out_shape=jax.ShapeDtypeStruct((t, d), F32),
        scratch_shapes=[pltpu.VMEM((tm, d), BF16), pltpu.VMEM((tm, d), F32)],
        compiler_params=_cparams("arbitrary", "arbitrary"),
        name="moe",
    )(x2d, gates, w1, w3, w2, lg, lb)


def _shared_kv_kernel(x_ref, wdkv_ref, gkv_ref, wkr_ref, wkrs_ref, cos_ref, sin_ref,
                      ckv_ref, kpe_ref, ckvb_ref, kpeb_ref):
    xb = x_ref[...].astype(BF16)
    ckv = _rms_norm(_dot(xb, wdkv_ref[...]), gkv_ref[...])
    kpe = _dot(xb, wkr_ref[...]) * cos_ref[...] + _dot(xb, wkrs_ref[...]) * sin_ref[...]
    ckv_ref[...] = ckv
    kpe_ref[...] = kpe[:, :QK_ROPE]
    ckvb_ref[...] = ckv.astype(BF16)
    kpeb_ref[...] = kpe.astype(BF16)


def _shared_kv(x2d, wdkv, gkv, wkr, wkrs, cos, sin, *, tm, table_blocks):
    t, d = x2d.shape
    tab = pl.BlockSpec((tm, ROPE_PAD), lambda i: (i % table_blocks, 0))
    return pl.pallas_call(
        _shared_kv_kernel,
        grid=(t // tm,),
        in_specs=[pl.BlockSpec((tm, d), lambda i: (i, 0)), _full(wdkv.shape), _full(gkv.shape),
                  _full(wkr.shape), _full(wkrs.shape), tab, tab],
        out_specs=[
            pl.BlockSpec((tm, KV_LORA), lambda i: (i, 0)),
            pl.BlockSpec((tm, QK_ROPE), lambda i: (i, 0)),
            pl.BlockSpec((tm, KV_LORA), lambda i: (i, 0)),
            pl.BlockSpec((tm, ROPE_PAD), lambda i: (i, 0)),
        ],
        out_shape=[
            jax.ShapeDtypeStruct((t, KV_LORA), F32),
            jax.ShapeDtypeStruct((t, QK_ROPE), F32),
            jax.ShapeDtypeStruct((t, KV_LORA), BF16),
            jax.ShapeDtypeStruct((t, ROPE_PAD), BF16),
        ],
        compiler_params=_cparams("arbitrary"),
        name="shared_kv",
    )(x2d, wdkv, gkv, wkr, wkrs, cos, sin)


def _project_queries(x, cos, sin, wdq, gq, wuqn, wuqp, wuqps, wukt_ref, store):
    cq = _rms_norm(_dot(x.astype(BF16), wdq), gq).astype(BF16)
    qn = _dot(cq, wuqn)
    qp = _dot(cq, wuqp)
    qps = _dot(cq, wuqps)
    for h in range(N_HEADS):
        lat = _dot(qn[:, h * QK_NOPE:(h + 1) * QK_NOPE].astype(BF16), wukt_ref[h])
        sl = slice(h * ROPE_PAD, (h + 1) * ROPE_PAD)
        pe = qp[:, sl] * cos + qps[:, sl] * sin
        store(h, lat.astype(BF16), pe.astype(BF16))


def _project_output(o_heads, x, wuv_ref, wo, lg, lb):
    o = jnp.concatenate([_dot(o_heads[h].astype(BF16), wuv_ref[h]) for h in range(N_HEADS)], axis=-1)
    mix = _dot(o.astype(BF16), wo)
    return _layer_norm(ALPHA * x + mix, lg, lb)


def _attn_prompt_kernel(x_ref, ckv_ref, kpe_ref, cos_ref, sin_ref, wdq_ref, gq_ref, wuqn_ref, wuqp_ref,
                        wuqps_ref, wukt_ref, wuv_ref, wo_ref, lg_ref, lb_ref, y_ref,
                        ql_ref, qp_ref, m_ref, l_ref, acc_ref, *, tq, tk):
    i = pl.program_id(1)
    x = x_ref[...]

    def store(h, lat, pe):
        ql_ref[h * tq:(h + 1) * tq, :] = lat
        qp_ref[h * tq:(h + 1) * tq, :] = pe

    _project_queries(x, cos_ref[...], sin_ref[...], wdq_ref[...], gq_ref[...], wuqn_ref[...], wuqp_ref[...],
                     wuqps_ref[...], wukt_ref, store)

    rows = N_HEADS * tq
    m_ref[...] = jnp.full((rows, 1), NEG_INF, F32)
    l_ref[...] = jnp.zeros((rows, 1), F32)
    acc_ref[...] = jnp.zeros((rows, KV_LORA), F32)

    def block(kb, masked):
        k0 = pl.multiple_of(kb * tk, tk)
        kc = ckv_ref[pl.ds(k0, tk), :]
        kp = kpe_ref[pl.ds(k0, tk), :]
        s = (_dot_nt(ql_ref[...], kc) + _dot_nt(qp_ref[...], kp)) * SCORE_SCALE
        if masked:
            r = lax.broadcasted_iota(jnp.int32, (rows, tk), 0) & (tq - 1)
            c = lax.broadcasted_iota(jnp.int32, (rows, tk), 1)
            s = jnp.where(k0 + c <= i * tq + r, s, NEG_INF)
        m_old = m_ref[...]
        m_new = jnp.maximum(m_old, jnp.max(s, axis=-1, keepdims=True))
        p = jnp.exp(s - m_new)
        scale = jnp.exp(m_old - m_new)
        l_ref[...] = scale * l_ref[...] + jnp.sum(p, axis=-1, keepdims=True)
        acc_ref[...] = scale * acc_ref[...] + _dot(p.astype(BF16), kc)
        m_ref[...] = m_new

    n_full = (i * tq) // tk

    def body(kb, carry):
        block(kb, False)
        return carry

    lax.fori_loop(0, n_full, body, 0)
    block(n_full, True)

    inv = 1.0 / l_ref[...]
    o_heads = [acc_ref[h * tq:(h + 1) * tq, :] * inv[h * tq:(h + 1) * tq] for h in range(N_HEADS)]
    y_ref[...] = _project_output(o_heads, x, wuv_ref, wo_ref[...], lg_ref[...], lb_ref[...])


def _attn_prompt(x, ckvb, kpeb, cos, sin, wdq, gq, wuqn, wuqp, wuqps, wukt, wuv, wo, lg, lb, *, tq=256, tk=512):
    b, s, d = x.shape
    assert tk % tq == 0 and s % tk == 0
    kern = functools.partial(_attn_prompt_kernel, tq=tq, tk=tk)
    rows = N_HEADS * tq
    return pl.pallas_call(
        kern,
        grid=(b, s // tq),
        in_specs=[
            pl.BlockSpec((None, tq, d), lambda i, j: (i, j, 0)),
            pl.BlockSpec((None, s, KV_LORA), lambda i, j: (i, 0, 0)),
            pl.BlockSpec((None, s, ROPE_PAD), lambda i, j: (i, 0, 0)),
            pl.BlockSpec((tq, ROPE_PAD), lambda i, j: (j, 0)),
            pl.BlockSpec((tq, ROPE_PAD), lambda i, j: (j, 0)),
            _full(wdq.shape), _full(gq.shape), _full(wuqn.shape), _full(wuqp.shape), _full(wuqps.shape),
            _full(wukt.shape), _full(wuv.shape), _full(wo.shape), _full(lg.shape), _full(lb.shape),
        ],
        out_specs=pl.BlockSpec((None, tq, d), lambda i, j: (i, j, 0)),
        out_shape=jax.ShapeDtypeStruct((b, s, d), F32),
        scratch_shapes=[
            pltpu.VMEM((rows, KV_LORA), BF16),
            pltpu.VMEM((rows, ROPE_PAD), BF16),
            pltpu.VMEM((rows, 1), F32),
            pltpu.VMEM((rows, 1), F32),
            pltpu.VMEM((rows, KV_LORA), F32),
        ],
        compiler_params=_cparams("arbitrary", "arbitrary"),
        name="attn_prompt",
    )(x, ckvb, kpeb, cos, sin, wdq, gq, wuqn, wuqp, wuqps, wukt, wuv, wo, lg, lb)


def _q_sample_kernel(x_ref, cos_ref, sin_ref, wdq_ref, gq_ref, wuqn_ref, wuqp_ref, wuqps_ref, wukt_ref,
                     ql_ref, qp_ref, *, bb, t):
    x = x_ref[...].reshape(bb * t, D_MODEL)

    def store(h, lat, pe):
        ql_ref[:, h] = lat.reshape(bb, t, KV_LORA)
        qp_ref[:, h] = pe.reshape(bb, t, ROPE_PAD)

    _project_queries(x, cos_ref[...], sin_ref[...], wdq_ref[...], gq_ref[...], wuqn_ref[...], wuqp_ref[...],
                     wuqps_ref[...], wukt_ref, store)


def _q_sample(x, cos, sin, wdq, gq, wuqn, wuqp, wuqps, wukt, *, bb=32):
    b, t, d = x.shape
    kern = functools.partial(_q_sample_kernel, bb=bb, t=t)
    return pl.pallas_call(
        kern,
        grid=(b // bb,),
        in_specs=[
            pl.BlockSpec((bb, t, d), lambda i: (i, 0, 0)),
            pl.BlockSpec((bb * t, ROPE_PAD), lambda i: (i, 0)),
            pl.BlockSpec((bb * t, ROPE_PAD), lambda i: (i, 0)),
            _full(wdq.shape), _full(gq.shape), _full(wuqn.shape), _full(wuqp.shape), _full(wuqps.shape),
            _full(wukt.shape),
        ],
        out_specs=[
            pl.BlockSpec((bb, N_HEADS, t, KV_LORA), lambda i: (i, 0, 0, 0)),
            pl.BlockSpec((bb, N_HEADS, t, ROPE_PAD), lambda i: (i, 0, 0, 0)),
        ],
        out_shape=[
            jax.ShapeDtypeStruct((b, N_HEADS, t, KV_LORA), BF16),
            jax.ShapeDtypeStruct((b, N_HEADS, t, ROPE_PAD), BF16),
        ],
        compiler_params=_cparams("arbitrary"),
        name="q_sample",
    )(x, cos, sin, wdq, gq, wuqn, wuqp, wuqps, wukt)


def _attn_sample_kernel(pt_ref, ql_ref, qp_ref, cnew_ref, pnew_ref, *refs, pages, page, t):
    ckv_pages = refs[:pages]
    kpe_pages = refs[pages:2 * pages]
    o_ref, kc_ref, kp_ref, m_ref, l_ref, acc_ref = refs[2 * pages:]
    j = pl.program_id(1)
    rows = ql_ref.shape[0]

    @pl.when(j == 0)
    def _():
        m_ref[...] = jnp.full((rows, 1), NEG_INF, F32)
        l_ref[...] = jnp.zeros((rows, 1), F32)
        acc_ref[...] = jnp.zeros((rows, KV_LORA), F32)

    for p in range(pages):
        kc_ref[p * page:(p + 1) * page, :] = ckv_pages[p][...].astype(BF16)
        kp_ref[:, p * page:(p + 1) * page] = kpe_pages[p][...].astype(BF16)

    ql = ql_ref[...]
    qp = qp_ref[...]

    def update(s, v):
        m_old = m_ref[...]
        m_new = jnp.maximum(m_old, jnp.max(s, axis=-1, keepdims=True))
        p = jnp.exp(s - m_new)
        scale = jnp.exp(m_old - m_new)
        l_ref[...] = scale * l_ref[...] + jnp.sum(p, axis=-1, keepdims=True)
        acc_ref[...] = scale * acc_ref[...] + _dot(p.astype(BF16), v)
        m_ref[...] = m_new

    kc = kc_ref[...]
    update((_dot_nt(ql, kc) + _dot(qp[:, :QK_ROPE], kp_ref[...])) * SCORE_SCALE, kc)

    @pl.when(j == pl.num_programs(1) - 1)
    def _():
        cn = cnew_ref[...]
        s = (_dot_nt(ql, cn) + _dot_nt(qp, pnew_ref[...])) * SCORE_SCALE
        n = cn.shape[0]
        qpos = lax.broadcasted_iota(jnp.int32, (rows, n), 0) & (t - 1)
        kpos = lax.broadcasted_iota(jnp.int32, (rows, n), 1)
        update(jnp.where(kpos <= qpos, s, NEG_INF), cn)
        o_ref[...] = acc_ref[...] * (1.0 / l_ref[...])


def _attn_sample(page_table, ql, qp, cnew, pnew, cache_ckv, cache_kpe_t, *, pages=16):
    b, rows, _ = ql.shape
    n_pages = page_table.shape[1]
    page = cache_ckv.shape[1]
    t = rows // N_HEADS
    n_new = cnew.shape[1]
    kern = functools.partial(_attn_sample_kernel, pages=pages, page=page, t=t)

    def page_spec(shape, p):
        return pl.BlockSpec((None,) + shape, lambda i, j, pt: (pt[i * n_pages + j * pages + p], 0, 0))

    grid_spec = pltpu.PrefetchScalarGridSpec(
        num_scalar_prefetch=1,
        grid=(b, n_pages // pages),
        in_specs=[
            pl.BlockSpec((None, rows, KV_LORA), lambda i, j, pt: (i, 0, 0)),
            pl.BlockSpec((None, rows, ROPE_PAD), lambda i, j, pt: (i, 0, 0)),
            pl.BlockSpec((None, n_new, KV_LORA), lambda i, j, pt: (i, 0, 0)),
            pl.BlockSpec((None, n_new, ROPE_PAD), lambda i, j, pt: (i, 0, 0)),
        ] + [page_spec((page, KV_LORA), p) for p in range(pages)]
          + [page_spec((QK_ROPE, page), p) for p in range(pages)],
        out_specs=pl.BlockSpec((None, rows, KV_LORA), lambda i, j, pt: (i, 0, 0)),
        scratch_shapes=[
            pltpu.VMEM((pages * page, KV_LORA), BF16),
            pltpu.VMEM((QK_ROPE, pages * page), BF16),
            pltpu.VMEM((rows, 1), F32),
            pltpu.VMEM((rows, 1), F32),
            pltpu.VMEM((rows, KV_LORA), F32),
        ],
    )
    return pl.pallas_call(
        kern,
        grid_spec=grid_spec,
        out_shape=jax.ShapeDtypeStruct((b, rows, KV_LORA), F32),
        compiler_params=_cparams("arbitrary", "arbitrary"),
        name="attn_sample",
    )(page_table.reshape(-1), ql, qp, cnew, pnew, *([cache_ckv] * pages), *([cache_kpe_t] * pages))


def _o_sample_kernel(o_ref, x_ref, wuv_ref, wo_ref, lg_ref, lb_ref, y_ref, *, bb, t):
    x = x_ref[...].reshape(bb * t, D_MODEL)
    o_heads = [o_ref[:, h].reshape(bb * t, KV_LORA) for h in range(N_HEADS)]
    y_ref[...] = _project_output(o_heads, x, wuv_ref, wo_ref[...], lg_ref[...], lb_ref[...]).reshape(bb, t, D_MODEL)


def _o_sample(o_lat, x, wuv, wo, lg, lb, *, bb=32):
    b, t, d = x.shape
    kern = functools.partial(_o_sample_kernel, bb=bb, t=t)
    return pl.pallas_call(
        kern,
        grid=(b // bb,),
        in_specs=[
            pl.BlockSpec((bb, N_HEADS, t, KV_LORA), lambda i: (i, 0, 0, 0)),
            pl.BlockSpec((bb, t, d), lambda i: (i, 0, 0)),
            _full(wuv.shape), _full(wo.shape), _full(lg.shape), _full(lb.shape),
        ],
        out_specs=pl.BlockSpec((bb, t, d), lambda i: (i, 0, 0)),
        out_shape=jax.ShapeDtypeStruct((b, t, d), F32),
        compiler_params=_cparams("arbitrary"),
        name="o_sample",
    )(o_lat, x, wuv, wo, lg, lb)


def _rope_tables(pos):
    half = QK_ROPE // 2
    inv_freq = ROPE_BASE ** (-jnp.arange(half, dtype=F32) / half)
    ang = pos.astype(F32)[:, None] * inv_freq[None, :]
    cos, sin = jnp.cos(ang), jnp.sin(ang)
    pad = jnp.zeros((pos.shape[0], ROPE_PAD - QK_ROPE), F32)
    return (jnp.concatenate([cos, cos, pad], axis=-1), jnp.concatenate([-sin, sin, pad], axis=-1))


def _swap_halves(w):
    half = QK_ROPE // 2
    return jnp.concatenate([w[..., half:], w[..., :half]], axis=-1)


def _pad_rope(w):
    return jnp.pad(w, [(0, 0)] * (w.ndim - 1) + [(0, ROPE_PAD - QK_ROPE)])


def _row(v):
    return v.reshape(1, -1)


def kernel(x_prompt, x_sample, state_conv, cache_ckv, cache_kpe, page_table, a_w_pw1, a_b_pw1, a_w_dw, a_b_dw, a_g_cn, a_b_cn, a_w_pw2, a_b_pw2, ln_mix_g, ln_mix_b, ln_ffn_g, ln_ffn_b, b_w_dq, b_g_q, b_w_uq, b_w_o, s_w_dkv, s_g_kv, s_w_kr, s_w_uk, s_w_uv, r_w, r_b, e_w1, e_w3, e_w2):
    bp, sp, d = x_prompt.shape
    bs, ts, _ = x_sample.shape
    past_len = page_table.shape[1] * cache_ckv.shape[1]

    a_w_pw1b, a_w_pw2b = a_w_pw1.astype(BF16), a_w_pw2.astype(BF16)
    w8 = jnp.broadcast_to(a_w_dw[:, :, None, :], (N_A_LAYERS, CONV_WIDTH, SUBLANES, d))
    e_w1b, e_w3b, e_w2b = e_w1.astype(BF16), e_w3.astype(BF16), e_w2.astype(BF16)
    perm = jnp.arange(N_EXPERTS).reshape(N_GROUPS, EXPERTS_PER_GROUP).T.reshape(-1)
    rwt = r_w.T[perm]
    rbp = r_b[perm].reshape(N_EXPERTS, 1)
    wdkv = s_w_dkv.astype(BF16)
    wkr = _pad_rope(s_w_kr).astype(BF16)
    wkrs = _pad_rope(_swap_halves(s_w_kr)).astype(BF16)
    wukt = jnp.transpose(s_w_uk, (1, 2, 0)).astype(BF16)
    wuv = jnp.transpose(s_w_uv, (1, 0, 2)).astype(BF16)
    wdq = b_w_dq.astype(BF16)
    uq_pe = b_w_uq[..., QK_NOPE:]
    wuqn = b_w_uq[..., :QK_NOPE].reshape(-1, Q_LORA, N_HEADS * QK_NOPE).astype(BF16)
    wuqp = _pad_rope(uq_pe).reshape(-1, Q_LORA, N_HEADS * ROPE_PAD).astype(BF16)
    wuqps = _pad_rope(_swap_halves(uq_pe)).reshape(-1, Q_LORA, N_HEADS * ROPE_PAD).astype(BF16)
    wo = b_w_o.astype(BF16)

    cos_p, sin_p = _rope_tables(jnp.arange(sp))
    cos_s1, sin_s1 = _rope_tables(past_len + jnp.arange(ts))
    cos_s, sin_s = jnp.tile(cos_s1, (bs, 1)), jnp.tile(sin_s1, (bs, 1))

    cls_lo = jnp.array([g * EXPERTS_PER_GROUP + lo for g in range(N_GROUPS) for lo, _ in MEMBER_PAIRS], jnp.int32)
    cls_hi = jnp.array([g * EXPERTS_PER_GROUP + hi for g in range(N_GROUPS) for _, hi in MEMBER_PAIRS], jnp.int32)

    def moe_block(x, l, routed):
        x2d = x.reshape(-1, d)
        t = x2d.shape[0]
        ffn = (e_w1b[l], e_w3b[l], e_w2b[l], _row(ln_ffn_g[l]), _row(ln_ffn_b[l]))
        gates_t, cls, xaug = _router(x2d, rwt, rbp)
        if not routed:
            gates = gates_t.reshape(EXPERTS_PER_GROUP, N_GROUPS, -1).transpose(2, 1, 0).reshape(-1, N_EXPERTS)
            return _moe(x2d, gates, *ffn).reshape(x.shape)
        pos, tile_cls, n_tiles = _positions(cls)
        rows = t + N_CLASSES * MOE_TILE
        nt = n_tiles[0, :1]
        tile = jnp.arange(rows // MOE_TILE)
        tc = jnp.clip(tile_cls[0, jnp.minimum(tile, nt[0] - 1)], 0, N_CLASSES - 1)
        pos3 = pos.reshape(-1, 1, 1024)
        xs = _row_scatter(pos3, xaug, jnp.zeros((rows, d + GATE_PAD), F32))
        ys = _moe_sorted(cls_lo[tc], cls_hi[tc], nt, xs, *ffn)
        return _row_gather(pos3, ys).reshape(x.shape)

    def conv_args(l):
        return (a_w_pw1b[l], _row(a_b_pw1[l]), w8[l], _row(a_b_dw[l]), _row(a_g_cn[l]), _row(a_b_cn[l]),
                a_w_pw2b[l], _row(a_b_pw2[l]), _row(ln_mix_g[l]), _row(ln_mix_b[l]))

    def q_args(j):
        return (wdq[j], _row(b_g_q[j]), wuqn[j], wuqp[j], wuqps[j], wukt)

    x = x_prompt
    conv_prompt = []
    for l in range(N_A_LAYERS):
        x, st = _conv_prompt(x, *conv_args(l))
        conv_prompt.append(st)
        x = moe_block(x, l, True)
    ckv_p, kpe_p, ckvb_p, kpeb_p = _shared_kv(x.reshape(-1, d), wdkv, _row(s_g_kv), wkr, wkrs, cos_p, sin_p,
                                              tm=512, table_blocks=sp // 512)
    ckvb_p3, kpeb_p3 = ckvb_p.reshape(bp, sp, KV_LORA), kpeb_p.reshape(bp, sp, ROPE_PAD)
    for l in range(N_A_LAYERS, DEPTH):
        j = l - N_A_LAYERS
        x = _attn_prompt(x, ckvb_p3, kpeb_p3, cos_p, sin_p, *q_args(j), wuv, wo[j],
                         _row(ln_mix_g[l]), _row(ln_mix_b[l]))
        x = moe_block(x, l, True)
    y_prompt = x

    x = x_sample
    conv_sample = []
    for l in range(N_A_LAYERS):
        x, st = _conv_sample(x, state_conv[l], *conv_args(l))
        conv_sample.append(st)
        x = moe_block(x, l, False)
    ckv_s, kpe_s, ckvb_s, kpeb_s = _shared_kv(x.reshape(-1, d), wdkv, _row(s_g_kv), wkr, wkrs, cos_s, sin_s,
                                              tm=512, table_blocks=(bs * ts) // 512)
    cache_kpe_t = jnp.transpose(cache_kpe, (0, 2, 1))
    new_rows = LANES
    cnew =jnp.pad(ckvb_s.reshape(bs, ts, KV_LORA), ((0, 0), (0, new_rows - ts), (0, 0)))
    pnew = jnp.pad(kpeb_s.reshape(bs, ts, ROPE_PAD), ((0, 0), (0, new_rows - ts), (0, 0)))
    for l in range(N_A_LAYERS, DEPTH):
        j = l - N_A_LAYERS
        ql, qp = _q_sample(x, cos_s, sin_s, *q_args(j))
        o_lat = _attn_sample(page_table, ql.reshape(bs, N_HEADS * ts, KV_LORA),
                             qp.reshape(bs, N_HEADS * ts, ROPE_PAD), cnew, pnew, cache_ckv, cache_kpe_t)
        x = _o_sample(o_lat.reshape(bs, N_HEADS, ts, KV_LORA), x, wuv, wo[j],
                      _row(ln_mix_g[l]), _row(ln_mix_b[l]))
        x = moe_block(x, l, False)
    y_sample = x

    return (y_prompt, y_sample, jnp.stack(conv_prompt), jnp.stack(conv_sample),
            ckv_p.reshape(bp, sp, KV_LORA), kpe_p.reshape(bp, sp, QK_ROPE),
            ckv_s.reshape(bs, ts, KV_LORA), kpe_s.reshape(bs, ts, QK_ROPE))
```

```python
import functools

import jax
import jax.numpy as jnp
from jax import lax
from jax.experimental import pallas as pl
from jax.experimental.pallas import tpu as pltpu

D_MODEL = 1024
DEPTH = 4
N_A_LAYERS = DEPTH // 2
CONV_WIDTH = 31
CONV_STATE = CONV_WIDTH - 1
N_HEADS = 8
QK_NOPE = 128
QK_ROPE = 64
V_HEAD = 128
KV_LORA = 256
Q_LORA = 512
ROPE_BASE = 10000.0
SCORE_SCALE = (QK_NOPE + QK_ROPE) ** -0.5
N_EXPERTS = 16
N_GROUPS = 4
EXPERTS_PER_GROUP = N_EXPERTS // N_GROUPS
D_EXPERT = 512
ALPHA = (2 * DEPTH) ** 0.25
LN_EPS = 1e-5
RMS_EPS = 1e-6

LANES = 128
SUBLANES = 8
VMEM_LIMIT_BYTES = 56 * 1024 * 1024

ROPE_PAD = LANES
HIST_ROWS = 32
HIST_SKIP = HIST_ROWS - CONV_STATE

MEMBER_PAIRS = ((0, 1), (0, 2), (0, 3), (1, 2), (1, 3), (2, 3))
N_CLASSES = N_GROUPS * len(MEMBER_PAIRS)
CLASS_ROWS = 32
GATE_PAD = LANES
MOE_TILE = 512

BF16 = jnp.bfloat16
F32 = jnp.float32
NEG_INF = float("-inf")


def _cparams(*sem):
    return pltpu.CompilerParams(dimension_semantics=sem, vmem_limit_bytes=VMEM_LIMIT_BYTES)


def _dot(a, b):
    return jnp.dot(a, b, preferred_element_type=F32)


def _dot_nt(a, b):
    return lax.dot_general(a, b, (((1,), (1,)), ((), ())), preferred_element_type=F32)


def _layer_norm(v, g, b):
    mu = jnp.mean(v, axis=-1, keepdims=True)
    d = v - mu
    var = jnp.mean(d * d, axis=-1, keepdims=True)
    return d * lax.rsqrt(var + LN_EPS) * g + b


def _rms_norm(v, g):
    return v * lax.rsqrt(jnp.mean(v * v, axis=-1, keepdims=True) + RMS_EPS) * g


def _sigmoid(v):
    return 1.0 / (1.0 + jnp.exp(-v))


def _full(shape):
    n = len(shape)
    return pl.BlockSpec(shape, lambda *_: (0,) * n)


def _conv_tail(conv, x, gcn, bcn, wpw2, bpw2, lg, lb):
    z = _layer_norm(conv, gcn, bcn)
    z = z * _sigmoid(z)
    mix = _dot(z.astype(BF16), wpw2) + bpw2
    return _layer_norm(ALPHA * x + mix, lg, lb)


def _conv_prompt_kernel(x_ref, wpw1_ref, bpw1_ref, w8_ref, bdw_ref, gcn_ref, bcn_ref, wpw2_ref, bpw2_ref,
                        lg_ref, lb_ref, y_ref, state_ref, ext_ref, conv_ref, *, ts, row_chunk, lane_chunk):
    s = pl.program_id(1)
    base = ext_ref.at[0]

    @pl.when(s == 0)
    def _():
        base[0:HIST_ROWS, :] = jnp.zeros((HIST_ROWS, D_MODEL), F32)

    @pl.when(s > 0)
    def _():
        base[0:HIST_ROWS, :] = base[ts:ts + HIST_ROWS, :]

    x = x_ref[...]
    h = _dot(x.astype(BF16), wpw1_ref[...]) + bpw1_ref[...]
    base[HIST_ROWS:HIST_ROWS + ts, :] = h[:, :D_MODEL] * _sigmoid(h[:, D_MODEL:])

    span = ts + HIST_ROWS - SUBLANES
    for j in range(1, SUBLANES):
        ext_ref[j, 0:span, :] = base[j:j + span, :]

    groups = row_chunk // SUBLANES

    def rows(r, carry):
        r0 = pl.multiple_of(r * row_chunk, row_chunk)
        for l0 in range(0, D_MODEL, lane_chunk):
            acc = jnp.broadcast_to(bdw_ref[:, l0:l0 + lane_chunk][None], (groups, SUBLANES, lane_chunk))
            for k in range(CONV_WIDTH):
                off = HIST_SKIP + k
                j, a = off % SUBLANES, off // SUBLANES
                start = pl.multiple_of(r0 + SUBLANES * a, SUBLANES)
                blk = ext_ref[j, pl.ds(start, row_chunk), l0:l0 + lane_chunk]
                acc = acc + blk.reshape(groups, SUBLANES, lane_chunk) * w8_ref[k, :, l0:l0 + lane_chunk][None]
            conv_ref[pl.ds(r0, row_chunk), l0:l0 + lane_chunk] = acc.reshape(row_chunk, lane_chunk)
        return carry

    lax.fori_loop(0, ts // row_chunk, rows, 0)

    y_ref[...] = _conv_tail(conv_ref[...], x, gcn_ref[...], bcn_ref[...], wpw2_ref[...], bpw2_ref[...],
                            lg_ref[...], lb_ref[...])

    @pl.when(s == pl.num_programs(1) - 1)
    def _():
        state_ref[...] = base[ts + HIST_SKIP:ts + HIST_ROWS, :]


def _conv_prompt(x, wpw1, bpw1, w8, bdw, gcn, bcn, wpw2, bpw2, lg, lb, *, ts=256):
    b, s, d = x.shape
    kern = functools.partial(_conv_prompt_kernel, ts=ts, row_chunk=32, lane_chunk=256)
    return pl.pallas_call(
        kern,
        grid=(b, s // ts),
        in_specs=[
            pl.BlockSpec((None, ts, d), lambda i, j: (i, j, 0)),
            _full(wpw1.shape), _full(bpw1.shape), _full(w8.shape), _full(bdw.shape), _full(gcn.shape),
            _full(bcn.shape), _full(wpw2.shape), _full(bpw2.shape), _full(lg.shape), _full(lb.shape),
        ],
        out_specs=[
            pl.BlockSpec((None, ts, d), lambda i, j: (i, j, 0)),
            pl.BlockSpec((None, CONV_STATE, d), lambda i, j: (i, 0, 0)),
        ],
        out_shape=[
            jax.ShapeDtypeStruct((b, s, d), F32),
            jax.ShapeDtypeStruct((b, CONV_STATE, d), F32),
        ],
        scratch_shapes=[
            pltpu.VMEM((SUBLANES, ts + HIST_ROWS, d), F32),
            pltpu.VMEM((ts, d), F32),
        ],
        compiler_params=_cparams("arbitrary", "arbitrary"),
        name="conv_prompt",
    )(x, wpw1, bpw1, w8, bdw, gcn, bcn, wpw2, bpw2, lg, lb)


def _conv_sample_kernel(x_ref, past_ref, wpw1_ref, bpw1_ref, w8_ref, bdw_ref, gcn_ref, bcn_ref, wpw2_ref,
                        bpw2_ref, lg_ref, lb_ref, y_ref, state_ref, ext_ref, conv_ref, *, bb, t, lane_chunk):
    x = x_ref[...].reshape(bb * t, D_MODEL)
    h = _dot(x.astype(BF16), wpw1_ref[...]) + bpw1_ref[...]
    u = h[:, :D_MODEL] * _sigmoid(h[:, D_MODEL:])
    ext_ref[:, HIST_SKIP:HIST_ROWS, :] = past_ref[...]
    ext_ref[:, HIST_ROWS:HIST_ROWS + t, :] = u.reshape(bb, t, D_MODEL)
    for l0 in range(0, D_MODEL, lane_chunk):
        acc = jnp.broadcast_to(bdw_ref[:, l0:l0 + lane_chunk][None], (bb, t, lane_chunk))
        for k in range(CONV_WIDTH):
            off = HIST_SKIP + k
            acc = acc + ext_ref[:, off:off + t, l0:l0 + lane_chunk] * w8_ref[k, :, l0:l0 + lane_chunk][None]
        conv_ref[:, :, l0:l0 + lane_chunk] = acc
    y = _conv_tail(conv_ref[...].reshape(bb * t, D_MODEL), x, gcn_ref[...], bcn_ref[...], wpw2_ref[...],
                   bpw2_ref[...], lg_ref[...], lb_ref[...])
    y_ref[...] = y.reshape(bb, t, D_MODEL)
    state_ref[...] = ext_ref[:, HIST_SKIP + t:HIST_ROWS + t, :]


def _conv_sample(x, past, wpw1, bpw1, w8, bdw, gcn, bcn, wpw2, bpw2, lg, lb, *, bb=16):
    b, t, d = x.shape
    assert t == SUBLANES
    kern = functools.partial(_conv_sample_kernel, bb=bb, t=t, lane_chunk=256)
    return pl.pallas_call(
        kern,
        grid=(b // bb,),
        in_specs=[
            pl.BlockSpec((bb, t, d), lambda i: (i, 0, 0)),
            pl.BlockSpec((bb, CONV_STATE, d), lambda i: (i, 0, 0)),
            _full(wpw1.shape), _full(bpw1.shape), _full(w8.shape), _full(bdw.shape), _full(gcn.shape),
            _full(bcn.shape), _full(wpw2.shape), _full(bpw2.shape), _full(lg.shape), _full(lb.shape),
        ],
        out_specs=[
            pl.BlockSpec((bb, t, d), lambda i: (i, 0, 0)),
            pl.BlockSpec((bb, CONV_STATE, d), lambda i: (i, 0, 0)),
        ],
        out_shape=[
            jax.ShapeDtypeStruct((b, t, d), F32),
            jax.ShapeDtypeStruct((b, CONV_STATE, d), F32),
        ],
        scratch_shapes=[
            pltpu.VMEM((bb, HIST_ROWS + t, d), F32),
            pltpu.VMEM((bb, t, d), F32),
        ],
        compiler_params=_cparams("arbitrary"),
        name="conv_sample",
    )(x, past, wpw1, bpw1, w8, bdw, gcn, bcn, wpw2, bpw2, lg, lb)


def _router_kernel(x_ref, rwt_ref, rb_ref, gates_ref, cls_ref, pair_ref):
    logits = lax.dot_general(rwt_ref[...], x_ref[...], (((1,), (1,)), ((), ())),
                             preferred_element_type=F32, precision=lax.Precision.HIGHEST)
    aff = _sigmoid(logits)
    sel = aff + rb_ref[...]
    g = N_GROUPS
    s = [sel[m * g:(m + 1) * g] for m in range(EXPERTS_PER_GROUP)]
    a = [aff[m * g:(m + 1) * g] for m in range(EXPERTS_PER_GROUP)]
    hi01, lo01 = jnp.maximum(s[0], s[1]), jnp.minimum(s[0], s[1])
    hi23, lo23 = jnp.maximum(s[2], s[3]), jnp.minimum(s[2], s[3])
    top1 = jnp.maximum(hi01, hi23)
    top2 = jnp.maximum(jnp.minimum(hi01, hi23), jnp.maximum(lo01, lo23))
    score = top1 + top2
    best = score[0:1]
    best_idx = jnp.zeros(best.shape, jnp.int32)
    for gi in range(1, g):
        better = score[gi:gi + 1] > best
        best = jnp.where(better, score[gi:gi + 1], best)
        best_idx = jnp.where(better, gi, best_idx)
    in_group = lax.broadcasted_iota(jnp.int32, score.shape, 0) == best_idx
    picked, chosen = [], []
    for m in range(EXPERTS_PER_GROUP):
        rank = jnp.zeros(score.shape, jnp.int32)
        for j in range(EXPERTS_PER_GROUP):
            if j == m:
                continue
            ahead = (s[j] >= s[m]) if j < m else (s[j] > s[m])
            rank = rank + ahead.astype(jnp.int32)
        keep = in_group & (rank < 2)
        picked.append(jnp.where(keep, a[m], 0.0))
        chosen.append(jnp.where(keep, 1.0, 0.0))
    p = [jnp.sum(v, axis=0, keepdims=True) for v in picked]
    on = [jnp.sum(v, axis=0, keepdims=True) > 0.5 for v in chosen]
    inv = 1.0 / (p[0] + p[1] + p[2] + p[3])
    for m in range(EXPERTS_PER_GROUP):
        gates_ref[m * g:(m + 1) * g, :] = picked[m] * inv
    lo = jnp.where(on[0], 0, jnp.where(on[1], 1, 2))
    hi = jnp.where(on[3], 3, jnp.where(on[2], 2, 1))
    p_lo = jnp.where(on[0], p[0], jnp.where(on[1], p[1], p[2]))
    p_hi = jnp.where(on[3], p[3], jnp.where(on[2], p[2], p[1]))
    pair = jnp.where(lo == 0, hi - 1, jnp.where(lo == 1, hi + 1, len(MEMBER_PAIRS) - 1))
    cls_ref[...] = best_idx * len(MEMBER_PAIRS) + pair
    pair_ref[...] = jnp.zeros(pair_ref.shape, F32)
    pair_ref[0:1, :] = p_lo * inv
    pair_ref[1:2, :] = p_hi * inv


def _router(x2d, rwt, rb, *, tm=1024):
    t, d = x2d.shape
    return pl.pallas_call(
        _router_kernel,
        grid=(t // tm,),
        in_specs=[pl.BlockSpec((tm, d), lambda i: (i, 0)), _full(rwt.shape), _full(rb.shape)],
        out_specs=[
            pl.BlockSpec((N_EXPERTS, tm), lambda i: (0, i)),
            pl.BlockSpec((1, tm), lambda i: (0, i)),
            pl.BlockSpec((SUBLANES, tm), lambda i: (0, i)),
        ],
        out_shape=[
            jax.ShapeDtypeStruct((N_EXPERTS, t), F32),
            jax.ShapeDtypeStruct((1, t), jnp.int32),
            jax.ShapeDtypeStruct((SUBLANES, t), F32),
        ],
        compiler_params=_cparams("arbitrary"),
        name="router",
    )(x2d, rwt, rb)


def _positions_kernel(cls_ref, pos_ref, tcls_ref, nt_ref, cnt_ref, start_ref, run_ref, *, tb, tm):
    ph = pl.program_id(0)
    i = pl.program_id(1)
    shift = tm.bit_length() - 1
    onehot = lax.broadcasted_iota(jnp.int32, (CLASS_ROWS, tb), 0) == cls_ref[...]
    ohf = jnp.where(onehot, 1.0, 0.0)

    @pl.when((ph == 0) & (i == 0))
    def _():
        cnt_ref[...] = jnp.zeros(cnt_ref.shape, F32)

    @pl.when(ph == 0)
    def _():
        cnt_ref[...] += jnp.sum(ohf, axis=1, keepdims=True)

    @pl.when((ph == 1) & (i == 0))
    def _():
        cnt = cnt_ref[...].astype(jnp.int32)
        padded = (((cnt + (tm - 1)) >> shift) << shift).astype(F32)
        r = lax.broadcasted_iota(jnp.int32, (CLASS_ROWS, CLASS_ROWS), 0)
        c = lax.broadcasted_iota(jnp.int32, (CLASS_ROWS, CLASS_ROWS), 1)
        start = jnp.dot(jnp.where(c < r, 1.0, 0.0), padded, preferred_element_type=F32,
                        precision=lax.Precision.HIGHEST)
        start_ref[...] = start
        run_ref[...] = jnp.zeros(run_ref.shape, F32)
        tile_start = (lax.broadcasted_iota(jnp.int32, (CLASS_ROWS, LANES), 1) << shift).astype(F32)
        real = lax.broadcasted_iota(jnp.int32, (CLASS_ROWS, LANES), 0) < N_CLASSES
        below = jnp.where(real, jnp.where(start <= tile_start, 1.0, 0.0), 0.0)
        tcls_ref[...] = jnp.sum(below, axis=0, keepdims=True).astype(jnp.int32) - 1
        nt_ref[...] = jnp.sum(padded, axis=0, keepdims=True).astype(jnp.int32) >> shift

    @pl.when(ph == 1)
    def _():
        rr = lax.broadcasted_iota(jnp.int32, (tb, tb), 0)
        cc = lax.broadcasted_iota(jnp.int32, (tb, tb), 1)
        upper = jnp.where(rr <= cc, 1.0, 0.0).astype(BF16)
        cum = _dot(ohf.astype(BF16), upper)
        base = run_ref[:, 0:1] + start_ref[:, 0:1] - 1.0
        pos_ref[...] = jnp.sum(ohf * (cum + base), axis=0, keepdims=True).astype(jnp.int32)
        run_ref[...] += cum[:, tb - 1:tb]


def _positions(cls, *, tb=512, tm=MOE_TILE):
    t = cls.shape[1]
    assert tm & (tm - 1) == 0 and (t + N_CLASSES * tm) // tm <= LANES
    kern = functools.partial(_positions_kernel, tb=tb, tm=tm)
    return pl.pallas_call(
        kern,
        grid=(2, t // tb),
        in_specs=[pl.BlockSpec((1, tb), lambda p, i: (0, i))],
        out_specs=[
            pl.BlockSpec((1, tb), lambda p, i: (0, i * p)),
            pl.BlockSpec((1, LANES), lambda p, i: (0, 0)),
            pl.BlockSpec((1, LANES), lambda p, i: (0, 0)),
        ],
        out_shape=[
            jax.ShapeDtypeStruct((1, t), jnp.int32),
            jax.ShapeDtypeStruct((1, LANES), jnp.int32),
            jax.ShapeDtypeStruct((1, LANES), jnp.int32),
        ],
        scratch_shapes=[pltpu.VMEM((CLASS_ROWS, LANES), F32)] * 3,
        compiler_params=_cparams("arbitrary", "arbitrary"),
        name="positions",
    )(cls)


def _row_copy(src, dst, src_row, dst_row, n, sem):
    return pltpu.make_async_copy(src.at[pl.ds(src_row, n), :], dst.at[pl.ds(dst_row, n), :], sem)


def _row_scatter_kernel(pos_ref, x_ref, pair_ref, init_hbm, dst_hbm, aug_ref, gcol_ref, sem, *, tb, unroll):
    del init_hbm
    aug_ref[:, :D_MODEL] = x_ref[...]
    gcol_ref[...] = jnp.zeros(gcol_ref.shape, F32)
    gcol_ref[0:SUBLANES, :] = pair_ref[...]
    aug_ref[:, D_MODEL:] = gcol_ref[...].T

    def issue(c, carry):
        for u in range(unroll):
            r = c * unroll + u
            _row_copy(aug_ref, dst_hbm, r, pos_ref[0, r], 1, sem).start(priority=u % 2)
        return carry

    lax.fori_loop(0, tb // unroll, issue, 0)
    _row_copy(aug_ref, dst_hbm, 0, 0, tb, sem).wait()


def _row_scatter(pos3, x2d, pair, init, *, unroll=8):
    nb, _, tb = pos3.shape
    t, d = x2d.shape
    kern = functools.partial(_row_scatter_kernel, tb=tb, unroll=unroll)
    return pl.pallas_call(
        kern,
        grid=(nb,),
        in_specs=[
            pl.BlockSpec((None, 1, tb), lambda i: (i, 0, 0), memory_space=pltpu.SMEM),
            pl.BlockSpec((tb, d), lambda i: (i, 0)),
            pl.BlockSpec((SUBLANES, tb), lambda i: (0, i)),
            pl.BlockSpec(memory_space=pl.ANY),
        ],
        out_specs=pl.BlockSpec(memory_space=pl.ANY),
        out_shape=jax.ShapeDtypeStruct(init.shape, init.dtype),
        scratch_shapes=[
            pltpu.VMEM((tb, d + GATE_PAD), F32),
            pltpu.VMEM((GATE_PAD, tb), F32),
            pltpu.SemaphoreType.DMA,
        ],
        input_output_aliases={3: 0},
        compiler_params=_cparams("arbitrary"),
        name="row_scatter",
    )(pos3, x2d, pair, init)


def _row_gather_kernel(pos_ref, src_hbm, y_ref, sem, *, tb, unroll):
    def issue(c, carry):
        for u in range(unroll):
            r = c * unroll + u
            _row_copy(src_hbm, y_ref, pos_ref[0, r], r, 1, sem).start(priority=u % 2)
        return carry

    lax.fori_loop(0, tb // unroll, issue, 0)
    _row_copy(src_hbm, y_ref, 0, 0, tb, sem).wait()


def _row_gather(pos3, src, *, unroll=8):
    nb, _, tb = pos3.shape
    d = src.shape[1]
    kern = functools.partial(_row_gather_kernel, tb=tb, unroll=unroll)
    return pl.pallas_call(
        kern,
        grid=(nb,),
        in_specs=[
            pl.BlockSpec((None, 1, tb), lambda i: (i, 0, 0), memory_space=pltpu.SMEM),
            pl.BlockSpec(memory_space=pl.ANY),
        ],
        out_specs=pl.BlockSpec((tb, d), lambda i: (i, 0)),
        out_shape=jax.ShapeDtypeStruct((nb * tb, d), src.dtype),
        scratch_shapes=[pltpu.SemaphoreType.DMA],
        compiler_params=_cparams("arbitrary"),
        name="row_gather",
    )(pos3, src)


def _moe_sorted_kernel(ea_ref, eb_ref, nt_ref, xs_ref, w1a_ref, w3a_ref, w2a_ref, w1b_ref, w3b_ref, w2b_ref,
                       lg_ref, lb_ref, y_ref):
    del ea_ref, eb_ref
    used = pl.program_id(0) < nt_ref[0]

    @pl.when(jnp.logical_not(used))
    def _():
        y_ref[...] = jnp.zeros(y_ref.shape, F32)

    @pl.when(used)
    def _():
        x = xs_ref[:, :D_MODEL]
        xb = x.astype(BF16)

        def ffn(w1_ref, w3_ref, w2_ref):
            h1 = _dot(xb, w1_ref[...])
            h3 = _dot(xb, w3_ref[...])
            return _dot((h1 * _sigmoid(h1) * h3).astype(BF16), w2_ref[...])

        out = xs_ref[:, D_MODEL:D_MODEL + 1] * ffn(w1a_ref, w3a_ref, w2a_ref)
        out = out + xs_ref[:, D_MODEL + 1:D_MODEL + 2] * ffn(w1b_ref, w3b_ref, w2b_ref)
        y_ref[...] = _layer_norm(ALPHA * x + out, lg_ref[...], lb_ref[...])


def _moe_sorted(ea, eb, nt, xs, w1, w3, w2, lg, lb, *, tm=MOE_TILE):
    p, da = xs.shape
    _, d, f = w1.shape

    def row_block(i, ea, eb, nt):
        return (i, 0)

    def wa(i, ea, eb, nt):
        return (ea[i], 0, 0)

    def wb(i, ea, eb, nt):
        return (eb[i], 0, 0)

    grid_spec = pltpu.PrefetchScalarGridSpec(
        num_scalar_prefetch=3,
        grid=(p // tm,),
        in_specs=[
            pl.BlockSpec((tm, da), row_block),
            pl.BlockSpec((None, d, f), wa), pl.BlockSpec((None, d, f), wa), pl.BlockSpec((None, f, d), wa),
            pl.BlockSpec((None, d, f), wb), pl.BlockSpec((None, d, f), wb), pl.BlockSpec((None, f, d), wb),
            pl.BlockSpec(lg.shape, lambda i, ea, eb, nt: (0, 0)),
            pl.BlockSpec(lb.shape, lambda i, ea, eb, nt: (0, 0)),
        ],
        out_specs=pl.BlockSpec((tm, d), row_block),
    )
    return pl.pallas_call(
        _moe_sorted_kernel,
        grid_spec=grid_spec,
        out_shape=jax.ShapeDtypeStruct((p, d), F32),
        compiler_params=_cparams("arbitrary"),
        name="moe_sorted",
    )(ea, eb, nt, xs, w1, w3, w2, w1, w3, w2, lg, lb)


def _moe_kernel(x_ref, gates_ref, w1_ref, w3_ref, w2_ref, lg_ref, lb_ref, y_ref, xb_ref, acc_ref):
    e = pl.program_id(1)

    @pl.when(e == 0)
    def _():
        xb_ref[...] = x_ref[...].astype(BF16)
        acc_ref[...] = jnp.zeros(acc_ref.shape, F32)

    xb = xb_ref[...]
    h1 = _dot(xb, w1_ref[...])
    h3 = _dot(xb, w3_ref[...])
    h = (h1 * _sigmoid(h1) * h3).astype(BF16)
    lane = lax.broadcasted_iota(jnp.int32, gates_ref.shape, 1)
    gate = jnp.sum(jnp.where(lane == e, gates_ref[...], 0.0), axis=1, keepdims=True)
    acc_ref[...] += gate * _dot(h, w2_ref[...])

    @pl.when(e == pl.num_programs(1) - 1)
    def _():
        y_ref[...] = _layer_norm(ALPHA * x_ref[...] + acc_ref[...], lg_ref[...], lb_ref[...])


def _moe(x2d, gates, w1, w3, w2, lg, lb, *, tm=1024):
    t, d = x2d.shape
    e, _, f = w1.shape
    return pl.pallas_call(
        _moe_kernel,
        grid=(t // tm, e),
        in_specs=[
            pl.BlockSpec((tm, d), lambda i, j: (i, 0)),
            pl.BlockSpec((tm, e), lambda i, j: (i, 0)),
            pl.BlockSpec((None, d, f), lambda i, j: (j, 0, 0)),
            pl.BlockSpec((None, d, f), lambda i, j: (j, 0, 0)),
            pl.BlockSpec((None, f, d), lambda i, j: (j, 0, 0)),
            _full(lg.shape), _full(lb.shape),
        ],
        out_specs=pl.BlockSpec((tm, d), lambda i, j: (i, 0)),
        out_shape=jax.ShapeDtypeStruct((t, d), F32),
        scratch_shapes=[pltpu.VMEM((tm, d), BF16), pltpu.VMEM((tm, d), F32)],
        compiler_params=_cparams("arbitrary", "arbitrary"),
        name="moe",
    )(x2d, gates, w1, w3, w2, lg, lb)


def _shared_kv_kernel(x_ref, wdkv_ref, gkv_ref, wkr_ref, wkrs_ref, cos_ref, sin_ref,
                      ckv_ref, kpe_ref, ckvb_ref, kpeb_ref):
    xb = x_ref[...].astype(BF16)
    ckv = _rms_norm(_dot(xb, wdkv_ref[...]), gkv_ref[...])
    kpe = _dot(xb, wkr_ref[...]) * cos_ref[...] + _dot(xb, wkrs_ref[...]) * sin_ref[...]
    ckv_ref[...] = ckv
    kpe_ref[...] = kpe[:, :QK_ROPE]
    ckvb_ref[...] = ckv.astype(BF16)
    kpeb_ref[...] = kpe.astype(BF16)


def _shared_kv(x2d, wdkv, gkv, wkr, wkrs, cos, sin, *, tm, table_blocks):
    t, d = x2d.shape
    tab = pl.BlockSpec((tm, ROPE_PAD), lambda i: (i % table_blocks, 0))
    return pl.pallas_call(
        _shared_kv_kernel,
        grid=(t // tm,),
        in_specs=[pl.BlockSpec((tm, d), lambda i: (i, 0)), _full(wdkv.shape), _full(gkv.shape),
                  _full(wkr.shape), _full(wkrs.shape), tab, tab],
        out_specs=[
            pl.BlockSpec((tm, KV_LORA), lambda i: (i, 0)),
            pl.BlockSpec((tm, QK_ROPE), lambda i: (i, 0)),
            pl.BlockSpec((tm, KV_LORA), lambda i: (i, 0)),
            pl.BlockSpec((tm, ROPE_PAD), lambda i: (i, 0)),
        ],
        out_shape=[
            jax.ShapeDtypeStruct((t, KV_LORA), F32),
            jax.ShapeDtypeStruct((t, QK_ROPE), F32),
            jax.ShapeDtypeStruct((t, KV_LORA), BF16),
            jax.ShapeDtypeStruct((t, ROPE_PAD), BF16),
        ],
        compiler_params=_cparams("arbitrary"),
        name="shared_kv",
    )(x2d, wdkv, gkv, wkr, wkrs, cos, sin)


def _project_queries(x, cos, sin, wdq, gq, wuqn, wuqp, wuqps, wukt_ref, store):
    cq = _rms_norm(_dot(x.astype(BF16), wdq), gq).astype(BF16)
    qn = _dot(cq, wuqn)
    qp = _dot(cq, wuqp)
    qps = _dot(cq, wuqps)
    cos, sin = cos * SCORE_SCALE, sin * SCORE_SCALE
    for h in range(N_HEADS):
        lat = _dot(qn[:, h * QK_NOPE:(h + 1) * QK_NOPE].astype(BF16), wukt_ref[h]) * SCORE_SCALE
        sl = slice(h * ROPE_PAD, (h + 1) * ROPE_PAD)
        pe = qp[:, sl] * cos + qps[:, sl] * sin
        store(h, lat.astype(BF16), pe.astype(BF16))


def _project_output(o_heads, x, wuv_ref, wo, lg, lb):
    o = jnp.concatenate([_dot(o_heads[h].astype(BF16), wuv_ref[h]) for h in range(N_HEADS)], axis=-1)
    mix = _dot(o.astype(BF16), wo)
    return _layer_norm(ALPHA * x + mix, lg, lb)


def _attn_prompt_kernel(x_ref, ckv_ref, kpe_ref, cos_ref, sin_ref, wdq_ref, gq_ref, wuqn_ref, wuqp_ref,
                        wuqps_ref, wukt_ref, wuv_ref, wo_ref, lg_ref, lb_ref, y_ref,
                        ql_ref, qp_ref, m_ref, l_ref, acc_ref, *, tq, tk, group_heads):
    i = pl.program_id(1)
    x = x_ref[...]

    def store(h, lat, pe):
        ql_ref[h * tq:(h + 1) * tq, :] = lat
        qp_ref[h * tq:(h + 1) * tq, :] = pe

    _project_queries(x, cos_ref[...], sin_ref[...], wdq_ref[...], gq_ref[...], wuqn_ref[...], wuqp_ref[...],
                     wuqps_ref[...], wukt_ref, store)

    rows = N_HEADS * tq
    m_ref[...] = jnp.full((rows, 1), NEG_INF, F32)
    l_ref[...] = jnp.zeros((rows, 1), F32)
    acc_ref[...] = jnp.zeros((rows, KV_LORA), F32)

    group = group_heads * tq

    def block(kb, masked):
        k0 = pl.multiple_of(kb * tk, tk)
        kc = ckv_ref[pl.ds(k0, tk), :]
        kp = kpe_ref[pl.ds(k0, tk), :]
        for r0 in range(0, rows, group):
            rs = slice(r0, r0 + group)
            s = _dot_nt(ql_ref[rs, :], kc) + _dot_nt(qp_ref[rs, :], kp)
            if masked:
                r = lax.broadcasted_iota(jnp.int32, (group, tk), 0) & (tq - 1)
                c = lax.broadcasted_iota(jnp.int32, (group, tk), 1)
                s = jnp.where(k0 + c <= i * tq + r, s, NEG_INF)
            m_old = m_ref[rs, :]
            m_new = jnp.maximum(m_old, jnp.max(s, axis=-1, keepdims=True))
            p = jnp.exp(s - m_new)
            scale = jnp.exp(m_old - m_new)
            l_ref[rs, :] = scale * l_ref[rs, :] + jnp.sum(p, axis=-1, keepdims=True)
            acc_ref[rs, :] = scale * acc_ref[rs, :] + _dot(p.astype(BF16), kc)
            m_ref[rs, :] = m_new

    n_full = (i * tq) // tk

    def body(kb, carry):
        block(kb, False)
        return carry

    lax.fori_loop(0, n_full, body, 0)
    block(n_full, True)

    inv = 1.0 / l_ref[...]
    o_heads = [acc_ref[h * tq:(h + 1) * tq, :] * inv[h * tq:(h + 1) * tq] for h in range(N_HEADS)]
    y_ref[...] = _project_output(o_heads, x, wuv_ref, wo_ref[...], lg_ref[...], lb_ref[...])


def _attn_prompt(x, ckvb, kpeb, cos, sin, wdq, gq, wuqn, wuqp, wuqps, wukt, wuv, wo, lg, lb, *, tq=256, tk=512,
                 group_heads=2):
    b, s, d = x.shape
    assert tk % tq == 0 and s % tk == 0 and tq & (tq - 1) == 0 and N_HEADS % group_heads == 0
    kern = functools.partial(_attn_prompt_kernel, tq=tq, tk=tk, group_heads=group_heads)
    rows = N_HEADS * tq
    return pl.pallas_call(
        kern,
        grid=(b, s // tq),
        in_specs=[
            pl.BlockSpec((None, tq, d), lambda i, j: (i, j, 0)),
            pl.BlockSpec((None, s, KV_LORA), lambda i, j: (i, 0, 0)),
            pl.BlockSpec((None, s, ROPE_PAD), lambda i, j: (i, 0, 0)),
            pl.BlockSpec((tq, ROPE_PAD), lambda i, j: (j, 0)),
            pl.BlockSpec((tq, ROPE_PAD), lambda i, j: (j, 0)),
            _full(wdq.shape), _full(gq.shape), _full(wuqn.shape), _full(wuqp.shape), _full(wuqps.shape),
            _full(wukt.shape), _full(wuv.shape), _full(wo.shape), _full(lg.shape), _full(lb.shape),
        ],
        out_specs=pl.BlockSpec((None, tq, d), lambda i, j: (i, j, 0)),
        out_shape=jax.ShapeDtypeStruct((b, s, d), F32),
        scratch_shapes=[
            pltpu.VMEM((rows, KV_LORA), BF16),
            pltpu.VMEM((rows, ROPE_PAD), BF16),
            pltpu.VMEM((rows, 1), F32),
            pltpu.VMEM((rows, 1), F32),
            pltpu.VMEM((rows, KV_LORA), F32),
        ],
        compiler_params=_cparams("arbitrary", "arbitrary"),
        name="attn_prompt",
    )(x, ckvb, kpeb, cos, sin, wdq, gq, wuqn, wuqp, wuqps, wukt, wuv, wo, lg, lb)


def _q_sample_kernel(x_ref, cos_ref, sin_ref, wdq_ref, gq_ref, wuqn_ref, wuqp_ref, wuqps_ref, wukt_ref,
                     ql_ref, qp_ref, *, bb, t):
    x = x_ref[...].reshape(bb * t, D_MODEL)

    def store(h, lat, pe):
        ql_ref[:, h] = lat.reshape(bb, t, KV_LORA)
        qp_ref[:, h] = pe.reshape(bb, t, ROPE_PAD)

    _project_queries(x, cos_ref[...], sin_ref[...], wdq_ref[...], gq_ref[...], wuqn_ref[...], wuqp_ref[...],
                     wuqps_ref[...], wukt_ref, store)


def _q_sample(x, cos, sin, wdq, gq, wuqn, wuqp, wuqps, wukt, *, bb=32):
    b, t, d = x.shape
    kern = functools.partial(_q_sample_kernel, bb=bb, t=t)
    return pl.pallas_call(
        kern,
        grid=(b // bb,),
        in_specs=[
            pl.BlockSpec((bb, t, d), lambda i: (i, 0, 0)),
            pl.BlockSpec((bb * t, ROPE_PAD), lambda i: (i, 0)),
            pl.BlockSpec((bb * t, ROPE_PAD), lambda i: (i, 0)),
            _full(wdq.shape), _full(gq.shape), _full(wuqn.shape), _full(wuqp.shape), _full(wuqps.shape),
            _full(wukt.shape),
        ],
        out_specs=[
            pl.BlockSpec((bb, N_HEADS, t, KV_LORA), lambda i: (i, 0, 0, 0)),
            pl.BlockSpec((bb, N_HEADS, t, ROPE_PAD), lambda i: (i, 0, 0, 0)),
        ],
        out_shape=[
            jax.ShapeDtypeStruct((b, N_HEADS, t, KV_LORA), BF16),
            jax.ShapeDtypeStruct((b, N_HEADS, t, ROPE_PAD), BF16),
        ],
        compiler_params=_cparams("arbitrary"),
        name="q_sample",
    )(x, cos, sin, wdq, gq, wuqn, wuqp, wuqps, wukt)


def _attn_sample_kernel(pt_ref, ql_ref, qp_ref, cnew_ref, pnew_ref, ckv_hbm, kpe_hbm, o_ref,
                        cbuf_ref, pbuf_ref, kc_ref, kp_ref, sem, *, pages, page, t):
    b = pl.program_id(0)
    nb = pl.num_programs(0)
    slot = b % 2
    rows = ql_ref.shape[0]

    def fetch(seq, into):
        for p in range(pages):
            src = pt_ref[seq * pages + p]
            pltpu.make_async_copy(ckv_hbm.at[src], cbuf_ref.at[into, p], sem.at[0, into]).start()
            pltpu.make_async_copy(kpe_hbm.at[src], pbuf_ref.at[into, p], sem.at[1, into]).start()

    @pl.when(b == 0)
    def _():
        fetch(0, 0)

    @pl.when(b + 1 < nb)
    def _():
        fetch(b + 1, 1 - slot)

    pltpu.make_async_copy(ckv_hbm.at[pl.ds(0, pages)], cbuf_ref.at[slot], sem.at[0, slot]).wait()
    pltpu.make_async_copy(kpe_hbm.at[pl.ds(0, pages)], pbuf_ref.at[slot], sem.at[1, slot]).wait()

    for p in range(pages):
        kc_ref[p * page:(p + 1) * page, :] = cbuf_ref[slot, p].astype(BF16)
        kp_ref[:, p * page:(p + 1) * page] = pbuf_ref[slot, p].astype(BF16)

    ql = ql_ref[...]
    qp = qp_ref[...]
    kc = kc_ref[...]
    cn = cnew_ref[...]
    s_past = _dot_nt(ql, kc) + _dot(qp[:, :QK_ROPE], kp_ref[...])
    s_new = _dot_nt(ql, cn) + _dot_nt(qp, pnew_ref[...])
    n = cn.shape[0]
    qpos = lax.broadcasted_iota(jnp.int32, (rows, n), 0) & (t - 1)
    kpos = lax.broadcasted_iota(jnp.int32, (rows, n), 1)
    s_new = jnp.where(kpos <= qpos, s_new, NEG_INF)
    m = jnp.maximum(jnp.max(s_past, axis=-1, keepdims=True), jnp.max(s_new, axis=-1, keepdims=True))
    p_past = jnp.exp(s_past - m)
    p_new = jnp.exp(s_new - m)
    l = jnp.sum(p_past, axis=-1, keepdims=True) + jnp.sum(p_new, axis=-1, keepdims=True)
    o_ref[...] = (_dot(p_past.astype(BF16), kc) + _dot(p_new.astype(BF16), cn)) * (1.0 / l)


def _attn_sample(page_table, ql, qp, cnew, pnew, cache_ckv, cache_kpe_t):
    b, rows, _ = ql.shape
    pages = page_table.shape[1]
    page = cache_ckv.shape[1]
    t = rows // N_HEADS
    n_new = cnew.shape[1]
    kern = functools.partial(_attn_sample_kernel, pages=pages, page=page, t=t)
    grid_spec = pltpu.PrefetchScalarGridSpec(
        num_scalar_prefetch=1,
        grid=(b,),
        in_specs=[
            pl.BlockSpec((None, rows, KV_LORA), lambda i, pt: (i, 0, 0)),
            pl.BlockSpec((None, rows, ROPE_PAD), lambda i, pt: (i, 0, 0)),
            pl.BlockSpec((None, n_new, KV_LORA), lambda i, pt: (i, 0, 0)),
            pl.BlockSpec((None, n_new, ROPE_PAD), lambda i, pt: (i, 0, 0)),
            pl.BlockSpec(memory_space=pl.ANY),
            pl.BlockSpec(memory_space=pl.ANY),
        ],
        out_specs=pl.BlockSpec((None, rows, KV_LORA), lambda i, pt: (i, 0, 0)),
        scratch_shapes=[
            pltpu.VMEM((2, pages, page, KV_LORA), F32),
            pltpu.VMEM((2, pages, QK_ROPE, page), F32),
            pltpu.VMEM((pages * page, KV_LORA), BF16),
            pltpu.VMEM((QK_ROPE, pages * page), BF16),
            pltpu.SemaphoreType.DMA((2, 2)),
        ],
    )
    return pl.pallas_call(
        kern,
        grid_spec=grid_spec,
        out_shape=jax.ShapeDtypeStruct((b, rows, KV_LORA), F32),
        compiler_params=_cparams("arbitrary"),
        name="attn_sample",
    )(page_table.reshape(-1), ql, qp, cnew, pnew, cache_ckv, cache_kpe_t)


def _o_sample_kernel(o_ref, x_ref, wuv_ref, wo_ref, lg_ref, lb_ref, y_ref, *, bb, t):
    x = x_ref[...].reshape(bb * t, D_MODEL)
    o_heads = [o_ref[:, h].reshape(bb * t, KV_LORA) for h in range(N_HEADS)]
    y_ref[...] = _project_output(o_heads, x, wuv_ref, wo_ref[...], lg_ref[...], lb_ref[...]).reshape(bb, t, D_MODEL)


def _o_sample(o_lat, x, wuv, wo, lg, lb, *, bb=32):
    b, t, d = x.shape
    kern = functools.partial(_o_sample_kernel, bb=bb, t=t)
    return pl.pallas_call(
        kern,
        grid=(b // bb,),
        in_specs=[
            pl.BlockSpec((bb, N_HEADS, t, KV_LORA), lambda i: (i, 0, 0, 0)),
            pl.BlockSpec((bb, t, d), lambda i: (i, 0, 0)),
            _full(wuv.shape), _full(wo.shape), _full(lg.shape), _full(lb.shape),
        ],
        out_specs=pl.BlockSpec((bb, t, d), lambda i: (i, 0, 0)),
        out_shape=jax.ShapeDtypeStruct((b, t, d), F32),
        compiler_params=_cparams("arbitrary"),
        name="o_sample",
    )(o_lat, x, wuv, wo, lg, lb)


def _rope_tables(pos):
    half = QK_ROPE // 2
    inv_freq = ROPE_BASE ** (-jnp.arange(half, dtype=F32) / half)
    ang = pos.astype(F32)[:, None] * inv_freq[None, :]
    cos, sin = jnp.cos(ang), jnp.sin(ang)
    pad = jnp.zeros((pos.shape[0], ROPE_PAD - QK_ROPE), F32)
    return (jnp.concatenate([cos, cos, pad], axis=-1), jnp.concatenate([-sin, sin, pad], axis=-1))


def _swap_halves(w):
    half = QK_ROPE // 2
    return jnp.concatenate([w[..., half:], w[..., :half]], axis=-1)


def _pad_rope(w):
    return jnp.pad(w, [(0, 0)] * (w.ndim - 1) + [(0, ROPE_PAD - QK_ROPE)])


def _row(v):
    return v.reshape(1, -1)


def kernel(x_prompt, x_sample, state_conv, cache_ckv, cache_kpe, page_table, a_w_pw1, a_b_pw1, a_w_dw, a_b_dw, a_g_cn, a_b_cn, a_w_pw2, a_b_pw2, ln_mix_g, ln_mix_b, ln_ffn_g, ln_ffn_b, b_w_dq, b_g_q, b_w_uq, b_w_o, s_w_dkv, s_g_kv, s_w_kr, s_w_uk, s_w_uv, r_w, r_b, e_w1, e_w3, e_w2):
    bp, sp, d = x_prompt.shape
    bs, ts, _ = x_sample.shape
    past_len = page_table.shape[1] * cache_ckv.shape[1]

    a_w_pw1b, a_w_pw2b = a_w_pw1.astype(BF16), a_w_pw2.astype(BF16)
    w8 = jnp.broadcast_to(a_w_dw[:, :, None, :], (N_A_LAYERS, CONV_WIDTH, SUBLANES, d))
    e_w1b, e_w3b, e_w2b = e_w1.astype(BF16), e_w3.astype(BF16), e_w2.astype(BF16)
    perm = jnp.arange(N_EXPERTS).reshape(N_GROUPS, EXPERTS_PER_GROUP).T.reshape(-1)
    rwt = r_w.T[perm]
    rbp = r_b[perm].reshape(N_EXPERTS, 1)
    wdkv = s_w_dkv.astype(BF16)
    wkr = _pad_rope(s_w_kr).astype(BF16)
    wkrs = _pad_rope(_swap_halves(s_w_kr)).astype(BF16)
    wukt = jnp.transpose(s_w_uk, (1, 2, 0)).astype(BF16)
    wuv = jnp.transpose(s_w_uv, (1, 0, 2)).astype(BF16)
    wdq = b_w_dq.astype(BF16)
    uq_pe = b_w_uq[..., QK_NOPE:]
    wuqn = b_w_uq[..., :QK_NOPE].reshape(-1, Q_LORA, N_HEADS * QK_NOPE).astype(BF16)
    wuqp = _pad_rope(uq_pe).reshape(-1, Q_LORA, N_HEADS * ROPE_PAD).astype(BF16)
    wuqps = _pad_rope(_swap_halves(uq_pe)).reshape(-1, Q_LORA, N_HEADS * ROPE_PAD).astype(BF16)
    wo = b_w_o.astype(BF16)

    cos_p, sin_p = _rope_tables(jnp.arange(sp))
    cos_s1, sin_s1 = _rope_tables(past_len + jnp.arange(ts))
    cos_s, sin_s = jnp.tile(cos_s1, (bs, 1)), jnp.tile(sin_s1, (bs, 1))

    cls_lo = jnp.array([g * EXPERTS_PER_GROUP + lo for g in range(N_GROUPS) for lo, _ in MEMBER_PAIRS], jnp.int32)
    cls_hi = jnp.array([g * EXPERTS_PER_GROUP + hi for g in range(N_GROUPS) for _, hi in MEMBER_PAIRS], jnp.int32)

    def moe_block(x, l, routed):
        x2d = x.reshape(-1, d)
        t = x2d.shape[0]
        ffn = (e_w1b[l], e_w3b[l], e_w2b[l], _row(ln_ffn_g[l]), _row(ln_ffn_b[l]))
        gates_t, cls, pair = _router(x2d, rwt, rbp)
        if not routed:
            gates = gates_t.reshape(EXPERTS_PER_GROUP, N_GROUPS, -1).transpose(2, 1, 0).reshape(-1, N_EXPERTS)
            return _moe(x2d, gates, *ffn).reshape(x.shape)
        pos, tile_cls, n_tiles = _positions(cls)
        rows = t + N_CLASSES * MOE_TILE
        nt = n_tiles[0, :1]
        tile = jnp.arange(rows // MOE_TILE)
        tc = jnp.clip(tile_cls[0, jnp.minimum(tile, nt[0] - 1)], 0, N_CLASSES - 1)
        pos3 = pos.reshape(-1, 1, 1024)
        xs = _row_scatter(pos3, x2d, pair, jnp.zeros((rows, d + GATE_PAD), F32))
        ys = _moe_sorted(cls_lo[tc], cls_hi[tc], nt, xs, *ffn)
        return _row_gather(pos3, ys).reshape(x.shape)

    def conv_args(l):
        return (a_w_pw1b[l], _row(a_b_pw1[l]), w8[l], _row(a_b_dw[l]), _row(a_g_cn[l]), _row(a_b_cn[l]),
                a_w_pw2b[l], _row(a_b_pw2[l]), _row(ln_mix_g[l]), _row(ln_mix_b[l]))

    def q_args(j):
        return (wdq[j], _row(b_g_q[j]), wuqn[j], wuqp[j], wuqps[j], wukt)

    x = x_prompt
    conv_prompt = []
    for l in range(N_A_LAYERS):
        x, st = _conv_prompt(x, *conv_args(l))
        conv_prompt.append(st)
        x = moe_block(x, l, True)
    ckv_p, kpe_p, ckvb_p, kpeb_p = _shared_kv(x.reshape(-1, d), wdkv, _row(s_g_kv), wkr, wkrs, cos_p, sin_p,
                                              tm=512, table_blocks=sp // 512)
    ckvb_p3, kpeb_p3 = ckvb_p.reshape(bp, sp, KV_LORA), kpeb_p.reshape(bp, sp, ROPE_PAD)
    for l in range(N_A_LAYERS, DEPTH):
        j = l - N_A_LAYERS
        x = _attn_prompt(x, ckvb_p3, kpeb_p3, cos_p, sin_p, *q_args(j), wuv, wo[j],
                         _row(ln_mix_g[l]), _row(ln_mix_b[l]))
        x = moe_block(x, l, True)
    y_prompt = x

    x = x_sample
    conv_sample = []
    for l in range(N_A_LAYERS):
        x, st = _conv_sample(x, state_conv[l], *conv_args(l))
        conv_sample.append(st)
        x = moe_block(x, l, False)
    ckv_s, kpe_s, ckvb_s, kpeb_s = _shared_kv(x.reshape(-1, d), wdkv, _row(s_g_kv), wkr, wkrs, cos_s, sin_s,
                                              tm=512, table_blocks=(bs * ts) // 512)
    cache_kpe_t = jnp.transpose(cache_kpe, (0, 2, 1))
    new_rows = LANES
    cnew =jnp.pad(ckvb_s.reshape(bs, ts, KV_LORA), ((0, 0), (0, new_rows - ts), (0, 0)))
    pnew = jnp.pad(kpeb_s.reshape(bs, ts, ROPE_PAD), ((0, 0), (0, new_rows - ts), (0, 0)))
    for l in range(N_A_LAYERS, DEPTH):
        j = l - N_A_LAYERS
        ql, qp = _q_sample(x, cos_s, sin_s, *q_args(j))
        o_lat = _attn_sample(page_table, ql.reshape(bs, N_HEADS * ts, KV_LORA),
                             qp.reshape(bs, N_HEADS * ts, ROPE_PAD), cnew, pnew, cache_ckv, cache_kpe_t)
        x = _o_sample(o_lat.reshape(bs, N_HEADS, ts, KV_LORA), x, wuv, wo[j],
                      _row(ln_mix_g[l]), _row(ln_mix_b[l]))
        x = moe_block(x, l, False)
    y_sample = x

    return (y_prompt, y_sample, jnp.stack(conv_prompt), jnp.stack(conv_sample),
            ckv_p.reshape(bp, sp, KV_LORA), kpe_p.reshape(bp, sp, QK_ROPE),
            ckv_s.reshape(bs, ts, KV_LORA), kpe_s.reshape(bs, ts, QK_ROPE))
```

```python
import functools

import jax
import jax.numpy as jnp
from jax import lax
from jax.experimental import pallas as pl
from jax.experimental.pallas import tpu as pltpu

D_MODEL = 1024
DEPTH = 4
N_A_LAYERS = DEPTH // 2
CONV_WIDTH = 31
CONV_STATE = CONV_WIDTH - 1
N_HEADS = 8
QK_NOPE = 128
QK_ROPE = 64
V_HEAD = 128
KV_LORA = 256
Q_LORA = 512
ROPE_BASE = 10000.0
SCORE_SCALE = (QK_NOPE + QK_ROPE) ** -0.5
N_EXPERTS = 16
N_GROUPS = 4
EXPERTS_PER_GROUP = N_EXPERTS // N_GROUPS
D_EXPERT = 512
ALPHA = (2 * DEPTH) ** 0.25
LN_EPS = 1e-5
RMS_EPS = 1e-6

LANES = 128
SUBLANES = 8
VMEM_LIMIT_BYTES = 56 * 1024 * 1024

ROPE_PAD = LANES
HIST_ROWS = 32
HIST_SKIP = HIST_ROWS - CONV_STATE

MEMBER_PAIRS = ((0, 1), (0, 2), (0, 3), (1, 2), (1, 3), (2, 3))
N_CLASSES = N_GROUPS * len(MEMBER_PAIRS)
CLASS_ROWS = 32
GATE_PAD = LANES
MOE_TILE = 512

BF16 = jnp.bfloat16
F32 = jnp.float32
NEG_INF = float("-inf")


def _cparams(*sem):
    return pltpu.CompilerParams(dimension_semantics=sem, vmem_limit_bytes=VMEM_LIMIT_BYTES)


def _dot(a, b):
    return jnp.dot(a, b, preferred_element_type=F32)


def _dot_nt(a, b):
    return lax.dot_general(a, b, (((1,), (1,)), ((), ())), preferred_element_type=F32)


def _layer_norm(v, g, b):
    mu = jnp.mean(v, axis=-1, keepdims=True)
    d = v - mu
    var = jnp.mean(d * d, axis=-1, keepdims=True)
    return d * lax.rsqrt(var + LN_EPS) * g + b


def _rms_norm(v, g):
    return v * lax.rsqrt(jnp.mean(v * v, axis=-1, keepdims=True) + RMS_EPS) * g


def _sigmoid(v):
    return 1.0 / (1.0 + jnp.exp(-v))


def _full(shape):
    n = len(shape)
    return pl.BlockSpec(shape, lambda *_: (0,) * n)


def _conv_tail(conv, x, gcn, bcn, wpw2, bpw2, lg, lb):
    z = _layer_norm(conv, gcn, bcn)
    z = z * _sigmoid(z)
    mix = _dot(z.astype(BF16), wpw2) + bpw2
    return _layer_norm(ALPHA * x + mix, lg, lb)


def _conv_prompt_kernel(x_ref, wpw1_ref, bpw1_ref, w8_ref, bdw_ref, gcn_ref, bcn_ref, wpw2_ref, bpw2_ref,
                        lg_ref, lb_ref, y_ref, state_ref, ext_ref, conv_ref, *, ts, row_chunk, lane_chunk):
    s = pl.program_id(1)
    base = ext_ref.at[0]

    @pl.when(s == 0)
    def _():
        base[0:HIST_ROWS, :] = jnp.zeros((HIST_ROWS, D_MODEL), F32)

    @pl.when(s > 0)
    def _():
        base[0:HIST_ROWS, :] = base[ts:ts + HIST_ROWS, :]

    x = x_ref[...]
    h = _dot(x.astype(BF16), wpw1_ref[...]) + bpw1_ref[...]
    base[HIST_ROWS:HIST_ROWS + ts, :] = h[:, :D_MODEL] * _sigmoid(h[:, D_MODEL:])

    span = ts + HIST_ROWS - SUBLANES
    for j in range(1, SUBLANES):
        ext_ref[j, 0:span, :] = base[j:j + span, :]

    groups = row_chunk // SUBLANES

    def rows(r, carry):
        r0 = pl.multiple_of(r * row_chunk, row_chunk)
        for l0 in range(0, D_MODEL, lane_chunk):
            acc = jnp.broadcast_to(bdw_ref[:, l0:l0 + lane_chunk][None], (groups, SUBLANES, lane_chunk))
            for k in range(CONV_WIDTH):
                off = HIST_SKIP + k
                j, a = off % SUBLANES, off // SUBLANES
                start = pl.multiple_of(r0 + SUBLANES * a, SUBLANES)
                blk = ext_ref[j, pl.ds(start, row_chunk), l0:l0 + lane_chunk]
                acc = acc + blk.reshape(groups, SUBLANES, lane_chunk) * w8_ref[k, :, l0:l0 + lane_chunk][None]
            conv_ref[pl.ds(r0, row_chunk), l0:l0 + lane_chunk] = acc.reshape(row_chunk, lane_chunk)
        return carry

    lax.fori_loop(0, ts // row_chunk, rows, 0)

    y_ref[...] = _conv_tail(conv_ref[...], x, gcn_ref[...], bcn_ref[...], wpw2_ref[...], bpw2_ref[...],
                            lg_ref[...], lb_ref[...])

    @pl.when(s == pl.num_programs(1) - 1)
    def _():
        state_ref[...] = base[ts + HIST_SKIP:ts + HIST_ROWS, :]


def _conv_prompt(x, wpw1, bpw1, w8, bdw, gcn, bcn, wpw2, bpw2, lg, lb, *, ts=256):
    b, s, d = x.shape
    kern = functools.partial(_conv_prompt_kernel, ts=ts, row_chunk=32, lane_chunk=256)
    return pl.pallas_call(
        kern,
        grid=(b, s // ts),
        in_specs=[
            pl.BlockSpec((None, ts, d), lambda i, j: (i, j, 0)),
            _full(wpw1.shape), _full(bpw1.shape), _full(w8.shape), _full(bdw.shape), _full(gcn.shape),
            _full(bcn.shape), _full(wpw2.shape), _full(bpw2.shape), _full(lg.shape), _full(lb.shape),
        ],
        out_specs=[
            pl.BlockSpec((None, ts, d), lambda i, j: (i, j, 0)),
            pl.BlockSpec((None, CONV_STATE, d), lambda i, j: (i, 0, 0)),
        ],
        out_shape=[
            jax.ShapeDtypeStruct((b, s, d), F32),
            jax.ShapeDtypeStruct((b, CONV_STATE, d), F32),
        ],
        scratch_shapes=[
            pltpu.VMEM((SUBLANES, ts + HIST_ROWS, d), F32),
            pltpu.VMEM((ts, d), F32),
        ],
        compiler_params=_cparams("arbitrary", "arbitrary"),
        name="conv_prompt",
    )(x, wpw1, bpw1, w8, bdw, gcn, bcn, wpw2, bpw2, lg, lb)


def _conv_sample_kernel(x_ref, past_ref, wpw1_ref, bpw1_ref, w8_ref, bdw_ref, gcn_ref, bcn_ref, wpw2_ref,
                        bpw2_ref, lg_ref, lb_ref, y_ref, state_ref, ext_ref, conv_ref, *, bb, t, lane_chunk):
    x = x_ref[...].reshape(bb * t, D_MODEL)
    h = _dot(x.astype(BF16), wpw1_ref[...]) + bpw1_ref[...]
    u = h[:, :D_MODEL] * _sigmoid(h[:, D_MODEL:])
    ext_ref[:, HIST_SKIP:HIST_ROWS, :] = past_ref[...]
    ext_ref[:, HIST_ROWS:HIST_ROWS + t, :] = u.reshape(bb, t, D_MODEL)
    for l0 in range(0, D_MODEL, lane_chunk):
        acc = jnp.broadcast_to(bdw_ref[:, l0:l0 + lane_chunk][None], (bb, t, lane_chunk))
        for k in range(CONV_WIDTH):
            off = HIST_SKIP + k
            acc = acc + ext_ref[:, off:off + t, l0:l0 + lane_chunk] * w8_ref[k, :, l0:l0 + lane_chunk][None]
        conv_ref[:, :, l0:l0 + lane_chunk] = acc
    y = _conv_tail(conv_ref[...].reshape(bb * t, D_MODEL), x, gcn_ref[...], bcn_ref[...], wpw2_ref[...],
                   bpw2_ref[...], lg_ref[...], lb_ref[...])
    y_ref[...] = y.reshape(bb, t, D_MODEL)
    state_ref[...] = ext_ref[:, HIST_SKIP + t:HIST_ROWS + t, :]


def _conv_sample(x, past, wpw1, bpw1, w8, bdw, gcn, bcn, wpw2, bpw2, lg, lb, *, bb=16):
    b, t, d = x.shape
    assert t == SUBLANES
    kern = functools.partial(_conv_sample_kernel, bb=bb, t=t, lane_chunk=256)
    return pl.pallas_call(
        kern,
        grid=(b // bb,),
        in_specs=[
            pl.BlockSpec((bb, t, d), lambda i: (i, 0, 0)),
            pl.BlockSpec((bb, CONV_STATE, d), lambda i: (i, 0, 0)),
            _full(wpw1.shape), _full(bpw1.shape), _full(w8.shape), _full(bdw.shape), _full(gcn.shape),
            _full(bcn.shape), _full(wpw2.shape), _full(bpw2.shape), _full(lg.shape), _full(lb.shape),
        ],
        out_specs=[
            pl.BlockSpec((bb, t, d), lambda i: (i, 0, 0)),
            pl.BlockSpec((bb, CONV_STATE, d), lambda i: (i, 0, 0)),
        ],
        out_shape=[
            jax.ShapeDtypeStruct((b, t, d), F32),
            jax.ShapeDtypeStruct((b, CONV_STATE, d), F32),
        ],
        scratch_shapes=[
            pltpu.VMEM((bb, HIST_ROWS + t, d), F32),
            pltpu.VMEM((bb, t, d), F32),
        ],
        compiler_params=_cparams("arbitrary"),
        name="conv_sample",
    )(x, past, wpw1, bpw1, w8, bdw, gcn, bcn, wpw2, bpw2, lg, lb)


def _router_kernel(x_ref, rwt_ref, rb_ref, gates_ref, cls_ref, pair_ref):
    x = x_ref[...]
    w = rwt_ref[...]
    x_hi = x.astype(BF16)
    x_lo = (x - x_hi.astype(F32)).astype(BF16)
    w_hi = w.astype(BF16)
    w_lo = (w - w_hi.astype(F32)).astype(BF16)
    by_hi = _dot_nt(jnp.concatenate([w_hi, w_lo], axis=0), x_hi)
    logits = by_hi[:N_EXPERTS] + by_hi[N_EXPERTS:] + _dot_nt(w_hi, x_lo)
    aff = _sigmoid(logits)
    sel = aff + rb_ref[...]
    g = N_GROUPS
    s = [sel[m * g:(m + 1) * g] for m in range(EXPERTS_PER_GROUP)]
    a = [aff[m * g:(m + 1) * g] for m in range(EXPERTS_PER_GROUP)]
    hi01, lo01 = jnp.maximum(s[0], s[1]), jnp.minimum(s[0], s[1])
    hi23, lo23 = jnp.maximum(s[2], s[3]), jnp.minimum(s[2], s[3])
    top1 = jnp.maximum(hi01, hi23)
    top2 = jnp.maximum(jnp.minimum(hi01, hi23), jnp.maximum(lo01, lo23))
    score = top1 + top2
    best = score[0:1]
    best_idx = jnp.zeros(best.shape, jnp.int32)
    for gi in range(1, g):
        better = score[gi:gi + 1] > best
        best = jnp.where(better, score[gi:gi + 1], best)
        best_idx = jnp.where(better, gi, best_idx)
    in_group = lax.broadcasted_iota(jnp.int32, score.shape, 0) == best_idx
    picked, chosen = [], []
    for m in range(EXPERTS_PER_GROUP):
        rank = jnp.zeros(score.shape, jnp.int32)
        for j in range(EXPERTS_PER_GROUP):
            if j == m:
                continue
            ahead = (s[j] >= s[m]) if j < m else (s[j] > s[m])
            rank = rank + ahead.astype(jnp.int32)
        keep = in_group & (rank < 2)
        picked.append(jnp.where(keep, a[m], 0.0))
        chosen.append(jnp.where(keep, 1.0, 0.0))
    p = [jnp.sum(v, axis=0, keepdims=True) for v in picked]
    on = [jnp.sum(v, axis=0, keepdims=True) > 0.5 for v in chosen]
    inv = 1.0 / (p[0] + p[1] + p[2] + p[3])
    for m in range(EXPERTS_PER_GROUP):
        gates_ref[m * g:(m + 1) * g, :] = picked[m] * inv
    lo = jnp.where(on[0], 0, jnp.where(on[1], 1, 2))
    hi = jnp.where(on[3], 3, jnp.where(on[2], 2, 1))
    p_lo = jnp.where(on[0], p[0], jnp.where(on[1], p[1], p[2]))
    p_hi = jnp.where(on[3], p[3], jnp.where(on[2], p[2], p[1]))
    pair = jnp.where(lo == 0, hi - 1, jnp.where(lo == 1, hi + 1, len(MEMBER_PAIRS) - 1))
    cls_ref[...] = best_idx * len(MEMBER_PAIRS) + pair
    pair_ref[...] = jnp.zeros(pair_ref.shape, F32)
    pair_ref[0:1, :] = p_lo * inv
    pair_ref[1:2, :] = p_hi * inv


def _router(x2d, rwt, rb, *, tm=1024):
    t, d = x2d.shape
    return pl.pallas_call(
        _router_kernel,
        grid=(t // tm,),
        in_specs=[pl.BlockSpec((tm, d), lambda i: (i, 0)), _full(rwt.shape), _full(rb.shape)],
        out_specs=[
            pl.BlockSpec((N_EXPERTS, tm), lambda i: (0, i)),
            pl.BlockSpec((1, tm), lambda i: (0, i)),
            pl.BlockSpec((SUBLANES, tm), lambda i: (0, i)),
        ],
        out_shape=[
            jax.ShapeDtypeStruct((N_EXPERTS, t), F32),
            jax.ShapeDtypeStruct((1, t), jnp.int32),
            jax.ShapeDtypeStruct((SUBLANES, t), F32),
        ],
        compiler_params=_cparams("arbitrary"),
        name="router",
    )(x2d, rwt, rb)


def _positions_kernel(cls_ref, pos_ref, tcls_ref, nt_ref, cnt_ref, start_ref, run_ref, *, tb, tm):
    ph = pl.program_id(0)
    i = pl.program_id(1)
    shift = tm.bit_length() - 1
    onehot = lax.broadcasted_iota(jnp.int32, (CLASS_ROWS, tb), 0) == cls_ref[...]
    ohf = jnp.where(onehot, 1.0, 0.0)

    @pl.when((ph == 0) & (i == 0))
    def _():
        cnt_ref[...] = jnp.zeros(cnt_ref.shape, F32)

    @pl.when(ph == 0)
    def _():
        cnt_ref[...] += jnp.sum(ohf, axis=1, keepdims=True)

    @pl.when((ph == 1) & (i == 0))
    def _():
        cnt = cnt_ref[...].astype(jnp.int32)
        padded = (((cnt + (tm - 1)) >> shift) << shift).astype(F32)
        r = lax.broadcasted_iota(jnp.int32, (CLASS_ROWS, CLASS_ROWS), 0)
        c = lax.broadcasted_iota(jnp.int32, (CLASS_ROWS, CLASS_ROWS), 1)
        start = jnp.dot(jnp.where(c < r, 1.0, 0.0), padded, preferred_element_type=F32,
                        precision=lax.Precision.HIGHEST)
        start_ref[...] = start
        run_ref[...] = jnp.zeros(run_ref.shape, F32)
        tile_start = (lax.broadcasted_iota(jnp.int32, (CLASS_ROWS, LANES), 1) << shift).astype(F32)
        real = lax.broadcasted_iota(jnp.int32, (CLASS_ROWS, LANES), 0) < N_CLASSES
        below = jnp.where(real, jnp.where(start <= tile_start, 1.0, 0.0), 0.0)
        tcls_ref[...] = jnp.sum(below, axis=0, keepdims=True).astype(jnp.int32) - 1
        nt_ref[...] = jnp.sum(padded, axis=0, keepdims=True).astype(jnp.int32) >> shift

    @pl.when(ph == 1)
    def _():
        rr = lax.broadcasted_iota(jnp.int32, (tb, tb), 0)
        cc = lax.broadcasted_iota(jnp.int32, (tb, tb), 1)
        upper = jnp.where(rr <= cc, 1.0, 0.0).astype(BF16)
        cum = _dot(ohf.astype(BF16), upper)
        base = run_ref[:, 0:1] + start_ref[:, 0:1] - 1.0
        pos_ref[...] = jnp.sum(ohf * (cum + base), axis=0, keepdims=True).astype(jnp.int32)
        run_ref[...] += cum[:, tb - 1:tb]


def _positions(cls, *, tb=512, tm=MOE_TILE):
    t = cls.shape[1]
    assert tm & (tm - 1) == 0 and (t + N_CLASSES * tm) // tm <= LANES
    kern = functools.partial(_positions_kernel, tb=tb, tm=tm)
    return pl.pallas_call(
        kern,
        grid=(2, t // tb),
        in_specs=[pl.BlockSpec((1, tb), lambda p, i: (0, i))],
        out_specs=[
            pl.BlockSpec((1, tb), lambda p, i: (0, i * p)),
            pl.BlockSpec((1, LANES), lambda p, i: (0, 0)),
            pl.BlockSpec((1, LANES), lambda p, i: (0, 0)),
        ],
        out_shape=[
            jax.ShapeDtypeStruct((1, t), jnp.int32),
            jax.ShapeDtypeStruct((1, LANES), jnp.int32),
            jax.ShapeDtypeStruct((1, LANES), jnp.int32),
        ],
        scratch_shapes=[pltpu.VMEM((CLASS_ROWS, LANES), F32)] * 3,
        compiler_params=_cparams("arbitrary", "arbitrary"),
        name="positions",
    )(cls)


def _staged_row(ref, c, u):
    return ref.at[c, pl.ds(u, 1), :]


def _hbm_row(ref, r):
    return ref.at[r >> (SUBLANES.bit_length() - 1), pl.ds(r & (SUBLANES - 1), 1), :]


def _row_scatter_kernel(pos_ref, x_ref, pair_ref, init_hbm, dst_hbm, aug_ref, gcol_ref, sem, *, tb):
    del init_hbm
    groups = tb // SUBLANES
    aug_ref[:, :, :D_MODEL] = x_ref[...]
    gcol_ref[...] = jnp.zeros(gcol_ref.shape, F32)
    gcol_ref[0:SUBLANES, :] = pair_ref[...]
    aug_ref[:, :, D_MODEL:] = gcol_ref[...].T.reshape(groups, SUBLANES, GATE_PAD)

    def issue(c, carry):
        for u in range(SUBLANES):
            dst = _hbm_row(dst_hbm, pos_ref[0, c * SUBLANES + u])
            pltpu.make_async_copy(_staged_row(aug_ref, c, u), dst, sem).start(priority=u % 2)
        return carry

    lax.fori_loop(0, groups, issue, 0)
    pltpu.make_async_copy(aug_ref, dst_hbm.at[pl.ds(0, groups)], sem).wait()


def _row_scatter(pos3, x2d, pair, init):
    nb, _, tb = pos3.shape
    t, d = x2d.shape
    rows, da = init.shape
    kern = functools.partial(_row_scatter_kernel, tb=tb)
    out = pl.pallas_call(
        kern,
        grid=(nb,),
        in_specs=[
            pl.BlockSpec((None, 1, tb), lambda i: (i, 0, 0), memory_space=pltpu.SMEM),
            pl.BlockSpec((tb // SUBLANES, SUBLANES, d), lambda i: (i, 0, 0)),
            pl.BlockSpec((SUBLANES, tb), lambda i: (0, i)),
            pl.BlockSpec(memory_space=pl.ANY),
        ],
        out_specs=pl.BlockSpec(memory_space=pl.ANY),
        out_shape=jax.ShapeDtypeStruct((rows // SUBLANES, SUBLANES, da), init.dtype),
        scratch_shapes=[
            pltpu.VMEM((tb // SUBLANES, SUBLANES, da), F32),
            pltpu.VMEM((GATE_PAD, tb), F32),
            pltpu.SemaphoreType.DMA,
        ],
        input_output_aliases={3: 0},
        compiler_params=_cparams("arbitrary"),
        name="row_scatter",
    )(pos3, x2d.reshape(t // SUBLANES, SUBLANES, d), pair, init.reshape(rows // SUBLANES, SUBLANES, da))
    return out.reshape(rows, da)


def _row_gather_kernel(pos_ref, src_hbm, y_ref, sem, *, tb):
    groups = tb // SUBLANES

    def issue(c, carry):
        for u in range(SUBLANES):
            src = _hbm_row(src_hbm, pos_ref[0, c * SUBLANES + u])
            pltpu.make_async_copy(src, _staged_row(y_ref, c, u), sem).start(priority=u % 2)
        return carry

    lax.fori_loop(0, groups, issue, 0)
    pltpu.make_async_copy(src_hbm.at[pl.ds(0, groups)], y_ref, sem).wait()


def _row_gather(pos3, src):
    nb, _, tb = pos3.shape
    rows, d = src.shape
    kern = functools.partial(_row_gather_kernel, tb=tb)
    out = pl.pallas_call(
        kern,
        grid=(nb,),
        in_specs=[
            pl.BlockSpec((None, 1, tb), lambda i: (i, 0, 0), memory_space=pltpu.SMEM),
            pl.BlockSpec(memory_space=pl.ANY),
        ],
        out_specs=pl.BlockSpec((tb // SUBLANES, SUBLANES, d), lambda i: (i, 0, 0)),
        out_shape=jax.ShapeDtypeStruct((nb * tb // SUBLANES, SUBLANES, d), src.dtype),
        scratch_shapes=[pltpu.SemaphoreType.DMA],
        compiler_params=_cparams("arbitrary"),
        name="row_gather",
    )(pos3, src.reshape(rows // SUBLANES, SUBLANES, d))
    return out.reshape(nb * tb, d)


def _moe_sorted_kernel(ea_ref, eb_ref, nt_ref, xs_ref, w1a_ref, w3a_ref, w2a_ref, w1b_ref, w3b_ref, w2b_ref,
                       lg_ref, lb_ref, y_ref, *wb_refs):
    i = pl.program_id(0)
    used = i < nt_ref[0]
    prev = jnp.maximum(i - 1, 0)
    new_pair = (i == 0) | (ea_ref[i] != ea_ref[prev]) | (eb_ref[i] != eb_ref[prev])

    @pl.when(jnp.logical_not(used))
    def _():
        y_ref[...] = jnp.zeros(y_ref.shape, F32)

    @pl.when(used & new_pair)
    def _():
        for src, dst in zip((w1a_ref, w3a_ref, w2a_ref, w1b_ref, w3b_ref, w2b_ref), wb_refs):
            dst[...] = src[...].astype(BF16)

    @pl.when(used)
    def _():
        x = xs_ref[:, :D_MODEL]
        xb = x.astype(BF16)

        def ffn(w1_ref, w3_ref, w2_ref):
            h1 = _dot(xb, w1_ref[...])
            h3 = _dot(xb, w3_ref[...])
            return _dot((h1 * _sigmoid(h1) * h3).astype(BF16), w2_ref[...])

        out = xs_ref[:, D_MODEL:D_MODEL + 1] * ffn(*wb_refs[:3])
        out = out + xs_ref[:, D_MODEL + 1:D_MODEL + 2] * ffn(*wb_refs[3:])
        y_ref[...] = _layer_norm(ALPHA * x + out, lg_ref[...], lb_ref[...])


def _moe_sorted(ea, eb, nt, xs, layer, w1, w3, w2, lg, lb, *, tm=MOE_TILE):
    p, da = xs.shape
    _, _, d, f = w1.shape

    def row_block(i, ea, eb, nt):
        return (i, 0)

    def wa(i, ea, eb, nt):
        return (layer, ea[i], 0, 0)

    def wb(i, ea, eb, nt):
        return (layer, eb[i], 0, 0)

    grid_spec = pltpu.PrefetchScalarGridSpec(
        num_scalar_prefetch=3,
        grid=(p // tm,),
        in_specs=[
            pl.BlockSpec((tm, da), row_block),
            pl.BlockSpec((None, None, d, f), wa), pl.BlockSpec((None, None, d, f), wa),
            pl.BlockSpec((None, None, f, d), wa),
            pl.BlockSpec((None, None, d, f), wb), pl.BlockSpec((None, None, d, f), wb),
            pl.BlockSpec((None, None, f, d), wb),
            pl.BlockSpec(lg.shape, lambda i, ea, eb, nt: (0, 0)),
            pl.BlockSpec(lb.shape, lambda i, ea, eb, nt: (0, 0)),
        ],
        out_specs=pl.BlockSpec((tm, d), row_block),
        scratch_shapes=[pltpu.VMEM((d, f), BF16), pltpu.VMEM((d, f), BF16), pltpu.VMEM((f, d), BF16)] * 2,
    )
    return pl.pallas_call(
        _moe_sorted_kernel,
        grid_spec=grid_spec,
        out_shape=jax.ShapeDtypeStruct((p, d), F32),
        compiler_params=_cparams("arbitrary"),
        name="moe_sorted",
    )(ea, eb, nt, xs, w1, w3, w2, w1, w3, w2, lg, lb)


def _moe_kernel(x_ref, gates_ref, w1_ref, w3_ref, w2_ref, lg_ref, lb_ref, y_ref, xb_ref, acc_ref):
    e = pl.program_id(1)

    @pl.when(e == 0)
    def _():
        xb_ref[...] = x_ref[...].astype(BF16)
        acc_ref[...] = jnp.zeros(acc_ref.shape, F32)

    xb = xb_ref[...]
    h1 = _dot(xb, w1_ref[...].astype(BF16))
    h3 = _dot(xb, w3_ref[...].astype(BF16))
    h = (h1 * _sigmoid(h1) * h3).astype(BF16)
    lane = lax.broadcasted_iota(jnp.int32, gates_ref.shape, 1)
    gate = jnp.sum(jnp.where(lane == e, gates_ref[...], 0.0), axis=1, keepdims=True)
    acc_ref[...] += gate * _dot(h, w2_ref[...].astype(BF16))

    @pl.when(e == pl.num_programs(1) - 1)
    def _():
        y_ref[...] = _layer_norm(ALPHA * x_ref[...] + acc_ref[...], lg_ref[...], lb_ref[...])


def _moe(x2d, gates, layer, w1, w3, w2, lg, lb, *, tm=1024):
    t, d = x2d.shape
    _, e, _, f = w1.shape
    return pl.pallas_call(
        _moe_kernel,
        grid=(t // tm, e),
        in_specs=[
            pl.BlockSpec((tm, d), lambda i, j: (i, 0)),
            pl.BlockSpec((tm, e), lambda i, j: (i, 0)),
            pl.BlockSpec((None, None, d, f), lambda i, j: (layer, j, 0, 0)),
            pl.BlockSpec((None, None, d, f), lambda i, j: (layer, j, 0, 0)),
            pl.BlockSpec((None, None, f, d), lambda i, j: (layer, j, 0, 0)),
            _full(lg.shape), _full(lb.shape),
        ],
        out_specs=pl.BlockSpec((tm, d), lambda i, j: (i, 0)),
        out_shape=jax.ShapeDtypeStruct((t, d), F32),
        scratch_shapes=[pltpu.VMEM((tm, d), BF16), pltpu.VMEM((tm, d), F32)],
        compiler_params=_cparams("arbitrary", "arbitrary"),
        name="moe",
    )(x2d, gates, w1, w3, w2, lg, lb)


def _shared_kv_kernel(x_ref, wdkv_ref, gkv_ref, wkr_ref, wkrs_ref, cos_ref, sin_ref,
                      ckv_ref, kpe_ref, ckvb_ref, kpeb_ref):
    xb = x_ref[...].astype(BF16)
    ckv = _rms_norm(_dot(xb, wdkv_ref[...]), gkv_ref[...])
    kpe = _dot(xb, wkr_ref[...]) * cos_ref[...] + _dot(xb, wkrs_ref[...]) * sin_ref[...]
    ckv_ref[...] = ckv
    kpe_ref[...] = kpe[:, :QK_ROPE]
    ckvb_ref[...] = ckv.astype(BF16)
    kpeb_ref[...] = kpe.astype(BF16)


def _shared_kv(x2d, wdkv, gkv, wkr, wkrs, cos, sin, *, tm, table_blocks):
    t, d = x2d.shape
    tab = pl.BlockSpec((tm, ROPE_PAD), lambda i: (i % table_blocks, 0))
    return pl.pallas_call(
        _shared_kv_kernel,
        grid=(t // tm,),
        in_specs=[pl.BlockSpec((tm, d), lambda i: (i, 0)), _full(wdkv.shape), _full(gkv.shape),
                  _full(wkr.shape), _full(wkrs.shape), tab, tab],
        out_specs=[
            pl.BlockSpec((tm, KV_LORA), lambda i: (i, 0)),
            pl.BlockSpec((tm, QK_ROPE), lambda i: (i, 0)),
            pl.BlockSpec((tm, KV_LORA), lambda i: (i, 0)),
            pl.BlockSpec((tm, ROPE_PAD), lambda i: (i, 0)),
        ],
        out_shape=[
            jax.ShapeDtypeStruct((t, KV_LORA), F32),
            jax.ShapeDtypeStruct((t, QK_ROPE), F32),
            jax.ShapeDtypeStruct((t, KV_LORA), BF16),
            jax.ShapeDtypeStruct((t, ROPE_PAD), BF16),
        ],
        compiler_params=_cparams("arbitrary"),
        name="shared_kv",
    )(x2d, wdkv, gkv, wkr, wkrs, cos, sin)


def _project_queries(x, cos, sin, wdq, gq, wuqn, wuqp, wuqps, wukt_ref, store):
    cq = _rms_norm(_dot(x.astype(BF16), wdq), gq).astype(BF16)
    qn = _dot(cq, wuqn)
    qp = _dot(cq, wuqp)
    qps = _dot(cq, wuqps)
    cos, sin = cos * SCORE_SCALE, sin * SCORE_SCALE
    for h in range(N_HEADS):
        lat = _dot(qn[:, h * QK_NOPE:(h + 1) * QK_NOPE].astype(BF16), wukt_ref[h]) * SCORE_SCALE
        sl = slice(h * ROPE_PAD, (h + 1) * ROPE_PAD)
        pe = qp[:, sl] * cos + qps[:, sl] * sin
        store(h, lat.astype(BF16), pe.astype(BF16))


def _project_output(o_heads, x, wuv_ref, wo, lg, lb):
    o = jnp.concatenate([_dot(o_heads[h].astype(BF16), wuv_ref[h]) for h in range(N_HEADS)], axis=-1)
    mix = _dot(o.astype(BF16), wo)
    return _layer_norm(ALPHA * x + mix, lg, lb)


def _attn_prompt_kernel(x_ref, ckv_ref, kpe_ref, cos_ref, sin_ref, wdq_ref, gq_ref, wuqn_ref, wuqp_ref,
                        wuqps_ref, wukt_ref, wuv_ref, wo_ref, lg_ref, lb_ref, y_ref,
                        ql_ref, qp_ref, m_ref, l_ref, acc_ref, *, tq, tk, group_heads):
    i = pl.program_id(1)
    x = x_ref[...]

    def store(h, lat, pe):
        ql_ref[h * tq:(h + 1) * tq, :] = lat
        qp_ref[h * tq:(h + 1) * tq, :] = pe

    _project_queries(x, cos_ref[...], sin_ref[...], wdq_ref[...], gq_ref[...], wuqn_ref[...], wuqp_ref[...],
                     wuqps_ref[...], wukt_ref, store)

    rows = N_HEADS * tq
    m_ref[...] = jnp.full((rows, 1), NEG_INF, F32)
    l_ref[...] = jnp.zeros((rows, 1), F32)
    acc_ref[...] = jnp.zeros((rows, KV_LORA), F32)

    group = group_heads * tq

    def block(kb, masked):
        k0 = pl.multiple_of(kb * tk, tk)
        kc = ckv_ref[pl.ds(k0, tk), :]
        kp = kpe_ref[pl.ds(k0, tk), :]
        for r0 in range(0, rows, group):
            rs = slice(r0, r0 + group)
            s = _dot_nt(ql_ref[rs, :], kc) + _dot_nt(qp_ref[rs, :], kp)
            if masked:
                r = lax.broadcasted_iota(jnp.int32, (group, tk), 0) & (tq - 1)
                c = lax.broadcasted_iota(jnp.int32, (group, tk), 1)
                s = jnp.where(k0 + c <= i * tq + r, s, NEG_INF)
            m_old = m_ref[rs, :]
            m_new = jnp.maximum(m_old, jnp.max(s, axis=-1, keepdims=True))
            p = jnp.exp(s - m_new)
            scale = jnp.exp(m_old - m_new)
            l_ref[rs, :] = scale * l_ref[rs, :] + jnp.sum(p, axis=-1, keepdims=True)
            acc_ref[rs, :] = scale * acc_ref[rs, :] + _dot(p.astype(BF16), kc)
            m_ref[rs, :] = m_new

    n_full = (i * tq) // tk

    def body(kb, carry):
        block(kb, False)
        return carry

    lax.fori_loop(0, n_full, body, 0)
    block(n_full, True)

    inv = 1.0 / l_ref[...]
    o_heads = [acc_ref[h * tq:(h + 1) * tq, :] * inv[h * tq:(h + 1) * tq] for h in range(N_HEADS)]
    y_ref[...] = _project_output(o_heads, x, wuv_ref, wo_ref[...], lg_ref[...], lb_ref[...])


def _attn_prompt(x, ckvb, kpeb, cos, sin, wdq, gq, wuqn, wuqp, wuqps, wukt, wuv, wo, lg, lb, *, tq=256, tk=512,
                 group_heads=2):
    b, s, d = x.shape
    assert tk % tq == 0 and s % tk == 0 and tq & (tq - 1) == 0 and N_HEADS % group_heads == 0
    kern = functools.partial(_attn_prompt_kernel, tq=tq, tk=tk, group_heads=group_heads)
    rows = N_HEADS * tq
    return pl.pallas_call(
        kern,
        grid=(b, s // tq),
        in_specs=[
            pl.BlockSpec((None, tq, d), lambda i, j: (i, j, 0)),
            pl.BlockSpec((None, s, KV_LORA), lambda i, j: (i, 0, 0)),
            pl.BlockSpec((None, s, ROPE_PAD), lambda i, j: (i, 0, 0)),
            pl.BlockSpec((tq, ROPE_PAD), lambda i, j: (j, 0)),
            pl.BlockSpec((tq, ROPE_PAD), lambda i, j: (j, 0)),
            _full(wdq.shape), _full(gq.shape), _full(wuqn.shape), _full(wuqp.shape), _full(wuqps.shape),
            _full(wukt.shape), _full(wuv.shape), _full(wo.shape), _full(lg.shape), _full(lb.shape),
        ],
        out_specs=pl.BlockSpec((None, tq, d), lambda i, j: (i, j, 0)),
        out_shape=jax.ShapeDtypeStruct((b, s, d), F32),
        scratch_shapes=[
            pltpu.VMEM((rows, KV_LORA), BF16),
            pltpu.VMEM((rows, ROPE_PAD), BF16),
            pltpu.VMEM((rows, 1), F32),
            pltpu.VMEM((rows, 1), F32),
            pltpu.VMEM((rows, KV_LORA), F32),
        ],
        compiler_params=_cparams("arbitrary", "arbitrary"),
        name="attn_prompt",
    )(x, ckvb, kpeb, cos, sin, wdq, gq, wuqn, wuqp, wuqps, wukt, wuv, wo, lg, lb)


def _q_sample_kernel(x_ref, cos_ref, sin_ref, wdq_ref, gq_ref, wuqn_ref, wuqp_ref, wuqps_ref, wukt_ref,
                     ql_ref, qp_ref, *, bb, t):
    x = x_ref[...].reshape(bb * t, D_MODEL)

    def store(h, lat, pe):
        ql_ref[:, h] = lat.reshape(bb, t, KV_LORA)
        qp_ref[:, h] = pe.reshape(bb, t, ROPE_PAD)

    _project_queries(x, cos_ref[...], sin_ref[...], wdq_ref[...], gq_ref[...], wuqn_ref[...], wuqp_ref[...],
                     wuqps_ref[...], wukt_ref, store)


def _q_sample(x, cos, sin, wdq, gq, wuqn, wuqp, wuqps, wukt, *, bb=32):
    b, t, d = x.shape
    kern = functools.partial(_q_sample_kernel, bb=bb, t=t)
    return pl.pallas_call(
        kern,
        grid=(b // bb,),
        in_specs=[
            pl.BlockSpec((bb, t, d), lambda i: (i, 0, 0)),
            pl.BlockSpec((bb * t, ROPE_PAD), lambda i: (i, 0)),
            pl.BlockSpec((bb * t, ROPE_PAD), lambda i: (i, 0)),
            _full(wdq.shape), _full(gq.shape), _full(wuqn.shape), _full(wuqp.shape), _full(wuqps.shape),
            _full(wukt.shape),
        ],
        out_specs=[
            pl.BlockSpec((bb, N_HEADS, t, KV_LORA), lambda i: (i, 0, 0, 0)),
            pl.BlockSpec((bb, N_HEADS, t, ROPE_PAD), lambda i: (i, 0, 0, 0)),
        ],
        out_shape=[
            jax.ShapeDtypeStruct((b, N_HEADS, t, KV_LORA), BF16),
            jax.ShapeDtypeStruct((b, N_HEADS, t, ROPE_PAD), BF16),
        ],
        compiler_params=_cparams("arbitrary"),
        name="q_sample",
    )(x, cos, sin, wdq, gq, wuqn, wuqp, wuqps, wukt)


def _attn_sample_kernel(pt_ref, ql_ref, qp_ref, cnew_ref, pnew_ref, ckv_hbm, kpe_hbm, o_ref,
                        cbuf_ref, pbuf_ref, kc_ref, kp_ref, sem, *, pages, page, t):
    b = pl.program_id(0)
    nb = pl.num_programs(0)
    slot = b % 2
    rows = ql_ref.shape[0]

    def fetch(seq, into):
        for p in range(pages):
            src = pt_ref[seq * pages + p]
            pltpu.make_async_copy(ckv_hbm.at[src], cbuf_ref.at[into, p], sem.at[0, into]).start()
            pltpu.make_async_copy(kpe_hbm.at[src], pbuf_ref.at[into, p], sem.at[1, into]).start()

    @pl.when(b == 0)
    def _():
        fetch(0, 0)

    @pl.when(b + 1 < nb)
    def _():
        fetch(b + 1, 1 - slot)

    pltpu.make_async_copy(ckv_hbm.at[pl.ds(0, pages)], cbuf_ref.at[slot], sem.at[0, slot]).wait()
    pltpu.make_async_copy(kpe_hbm.at[pl.ds(0, pages)], pbuf_ref.at[slot], sem.at[1, slot]).wait()

    for p in range(pages):
        kc_ref[p * page:(p + 1) * page, :] = cbuf_ref[slot, p].astype(BF16)
        kp_ref[:, p * page:(p + 1) * page] = pbuf_ref[slot, p].astype(BF16)

    ql = ql_ref[...]
    qp = qp_ref[...]
    kc = kc_ref[...]
    cn = cnew_ref[...]
    s_past = _dot_nt(ql, kc) + _dot(qp[:, :QK_ROPE], kp_ref[...])
    s_new = _dot_nt(ql, cn) + _dot_nt(qp, pnew_ref[...])
    n = cn.shape[0]
    qpos = lax.broadcasted_iota(jnp.int32, (rows, n), 0) & (t - 1)
    kpos = lax.broadcasted_iota(jnp.int32, (rows, n), 1)
    s_new = jnp.where(kpos <= qpos, s_new, NEG_INF)
    m = jnp.maximum(jnp.max(s_past, axis=-1, keepdims=True), jnp.max(s_new, axis=-1, keepdims=True))
    p_past = jnp.exp(s_past - m)
    p_new = jnp.exp(s_new - m)
    l = jnp.sum(p_past, axis=-1, keepdims=True) + jnp.sum(p_new, axis=-1, keepdims=True)
    o_ref[...] = (_dot(p_past.astype(BF16), kc) + _dot(p_new.astype(BF16), cn)) * (1.0 / l)


def _attn_sample(page_table, ql, qp, cnew, pnew, cache_ckv, cache_kpe_t):
    b, rows, _ = ql.shape
    pages = page_table.shape[1]
    page = cache_ckv.shape[1]
    t = rows // N_HEADS
    n_new = cnew.shape[1]
    kern = functools.partial(_attn_sample_kernel, pages=pages, page=page, t=t)
    grid_spec = pltpu.PrefetchScalarGridSpec(
        num_scalar_prefetch=1,
        grid=(b,),
        in_specs=[
            pl.BlockSpec((None, rows, KV_LORA), lambda i, pt: (i, 0, 0)),
            pl.BlockSpec((None, rows, ROPE_PAD), lambda i, pt: (i, 0, 0)),
            pl.BlockSpec((None, n_new, KV_LORA), lambda i, pt: (i, 0, 0)),
            pl.BlockSpec((None, n_new, ROPE_PAD), lambda i, pt: (i, 0, 0)),
            pl.BlockSpec(memory_space=pl.ANY),
            pl.BlockSpec(memory_space=pl.ANY),
        ],
        out_specs=pl.BlockSpec((None, rows, KV_LORA), lambda i, pt: (i, 0, 0)),
        scratch_shapes=[
            pltpu.VMEM((2, pages, page, KV_LORA), F32),
            pltpu.VMEM((2, pages, QK_ROPE, page), F32),
            pltpu.VMEM((pages * page, KV_LORA), BF16),
            pltpu.VMEM((QK_ROPE, pages * page), BF16),
            pltpu.SemaphoreType.DMA((2, 2)),
        ],
    )
    return pl.pallas_call(
        kern,
        grid_spec=grid_spec,
        out_shape=jax.ShapeDtypeStruct((b, rows, KV_LORA), F32),
        compiler_params=_cparams("arbitrary"),
        name="attn_sample",
    )(page_table.reshape(-1), ql, qp, cnew, pnew, cache_ckv, cache_kpe_t)


def _o_sample_kernel(o_ref, x_ref, wuv_ref, wo_ref, lg_ref, lb_ref, y_ref, *, bb, t):
    x = x_ref[...].reshape(bb * t, D_MODEL)
    o_heads = [o_ref[:, h].reshape(bb * t, KV_LORA) for h in range(N_HEADS)]
    y_ref[...] = _project_output(o_heads, x, wuv_ref, wo_ref[...], lg_ref[...], lb_ref[...]).reshape(bb, t, D_MODEL)


def _o_sample(o_lat, x, wuv, wo, lg, lb, *, bb=32):
    b, t, d = x.shape
    kern = functools.partial(_o_sample_kernel, bb=bb, t=t)
    return pl.pallas_call(
        kern,
        grid=(b // bb,),
        in_specs=[
            pl.BlockSpec((bb, N_HEADS, t, KV_LORA), lambda i: (i, 0, 0, 0)),
            pl.BlockSpec((bb, t, d), lambda i: (i, 0, 0)),
            _full(wuv.shape), _full(wo.shape), _full(lg.shape), _full(lb.shape),
        ],
        out_specs=pl.BlockSpec((bb, t, d), lambda i: (i, 0, 0)),
        out_shape=jax.ShapeDtypeStruct((b, t, d), F32),
        compiler_params=_cparams("arbitrary"),
        name="o_sample",
    )(o_lat, x, wuv, wo, lg, lb)


def _rope_tables(pos):
    half = QK_ROPE // 2
    inv_freq = ROPE_BASE ** (-jnp.arange(half, dtype=F32) / half)
    ang = pos.astype(F32)[:, None] * inv_freq[None, :]
    cos, sin = jnp.cos(ang), jnp.sin(ang)
    pad = jnp.zeros((pos.shape[0], ROPE_PAD - QK_ROPE), F32)
    return (jnp.concatenate([cos, cos, pad], axis=-1), jnp.concatenate([-sin, sin, pad], axis=-1))


def _swap_halves(w):
    half = QK_ROPE // 2
    return jnp.concatenate([w[..., half:], w[..., :half]], axis=-1)


def _pad_rope(w):
    return jnp.pad(w, [(0, 0)] * (w.ndim - 1) + [(0, ROPE_PAD - QK_ROPE)])


def _row(v):
    return v.reshape(1, -1)


def kernel(x_prompt, x_sample, state_conv, cache_ckv, cache_kpe, page_table, a_w_pw1, a_b_pw1, a_w_dw, a_b_dw, a_g_cn, a_b_cn, a_w_pw2, a_b_pw2, ln_mix_g, ln_mix_b, ln_ffn_g, ln_ffn_b, b_w_dq, b_g_q, b_w_uq, b_w_o, s_w_dkv, s_g_kv, s_w_kr, s_w_uk, s_w_uv, r_w, r_b, e_w1, e_w3, e_w2):
    bp, sp, d = x_prompt.shape
    bs, ts, _ = x_sample.shape
    past_len = page_table.shape[1] * cache_ckv.shape[1]

    a_w_pw1b, a_w_pw2b = a_w_pw1.astype(BF16), a_w_pw2.astype(BF16)
    w8 = jnp.broadcast_to(a_w_dw[:, :, None, :], (N_A_LAYERS, CONV_WIDTH, SUBLANES, d))
    perm = jnp.arange(N_EXPERTS).reshape(N_GROUPS, EXPERTS_PER_GROUP).T.reshape(-1)
    rwt = r_w.T[perm]
    rbp = r_b[perm].reshape(N_EXPERTS, 1)
    wdkv = s_w_dkv.astype(BF16)
    wkr = _pad_rope(s_w_kr).astype(BF16)
    wkrs = _pad_rope(_swap_halves(s_w_kr)).astype(BF16)
    wukt = jnp.transpose(s_w_uk, (1, 2, 0)).astype(BF16)
    wuv = jnp.transpose(s_w_uv, (1, 0, 2)).astype(BF16)
    wdq = b_w_dq.astype(BF16)
    uq_pe = b_w_uq[..., QK_NOPE:]
    wuqn = b_w_uq[..., :QK_NOPE].reshape(-1, Q_LORA, N_HEADS * QK_NOPE).astype(BF16)
    wuqp = _pad_rope(uq_pe).reshape(-1, Q_LORA, N_HEADS * ROPE_PAD).astype(BF16)
    wuqps = _pad_rope(_swap_halves(uq_pe)).reshape(-1, Q_LORA, N_HEADS * ROPE_PAD).astype(BF16)
    wo = b_w_o.astype(BF16)

    cos_p, sin_p = _rope_tables(jnp.arange(sp))
    cos_s1, sin_s1 = _rope_tables(past_len + jnp.arange(ts))
    cos_s, sin_s = jnp.tile(cos_s1, (bs, 1)), jnp.tile(sin_s1, (bs, 1))

    cls_lo = jnp.array([g * EXPERTS_PER_GROUP + lo for g in range(N_GROUPS) for lo, _ in MEMBER_PAIRS], jnp.int32)
    cls_hi = jnp.array([g * EXPERTS_PER_GROUP + hi for g in range(N_GROUPS) for _, hi in MEMBER_PAIRS], jnp.int32)

    def moe_block(x, l, routed):
        x2d = x.reshape(-1, d)
        t = x2d.shape[0]
        ffn = (l, e_w1, e_w3, e_w2, _row(ln_ffn_g[l]), _row(ln_ffn_b[l]))
        gates_t, cls, pair = _router(x2d, rwt, rbp)
        if not routed:
            gates = gates_t.reshape(EXPERTS_PER_GROUP, N_GROUPS, -1).transpose(2, 1, 0).reshape(-1, N_EXPERTS)
            return _moe(x2d, gates, *ffn).reshape(x.shape)
        pos, tile_cls, n_tiles = _positions(cls)
        rows = t + N_CLASSES * MOE_TILE
        nt = n_tiles[0, :1]
        tile = jnp.arange(rows // MOE_TILE)
        tc = jnp.clip(tile_cls[0, jnp.minimum(tile, nt[0] - 1)], 0, N_CLASSES - 1)
        pos3 = pos.reshape(-1, 1, 1024)
        xs = _row_scatter(pos3, x2d, pair, jnp.zeros((rows, d + GATE_PAD), F32))
        ys = _moe_sorted(cls_lo[tc], cls_hi[tc], nt, xs, *ffn)
        return _row_gather(pos3, ys).reshape(x.shape)

    def conv_args(l):
        return (a_w_pw1b[l], _row(a_b_pw1[l]), w8[l], _row(a_b_dw[l]), _row(a_g_cn[l]), _row(a_b_cn[l]),
                a_w_pw2b[l], _row(a_b_pw2[l]), _row(ln_mix_g[l]), _row(ln_mix_b[l]))

    def q_args(j):
        return (wdq[j], _row(b_g_q[j]), wuqn[j], wuqp[j], wuqps[j], wukt)

    x = x_prompt
    conv_prompt = []
    for l in range(N_A_LAYERS):
        x, st = _conv_prompt(x, *conv_args(l))
        conv_prompt.append(st)
        x = moe_block(x, l, True)
    ckv_p, kpe_p, ckvb_p, kpeb_p = _shared_kv(x.reshape(-1, d), wdkv, _row(s_g_kv), wkr, wkrs, cos_p, sin_p,
                                              tm=512, table_blocks=sp // 512)
    ckvb_p3, kpeb_p3 = ckvb_p.reshape(bp, sp, KV_LORA), kpeb_p.reshape(bp, sp, ROPE_PAD)
    for l in range(N_A_LAYERS, DEPTH):
        j = l - N_A_LAYERS
        x = _attn_prompt(x, ckvb_p3, kpeb_p3, cos_p, sin_p, *q_args(j), wuv, wo[j],
                         _row(ln_mix_g[l]), _row(ln_mix_b[l]))
        x = moe_block(x, l, True)
    y_prompt = x

    x = x_sample
    conv_sample = []
    for l in range(N_A_LAYERS):
        x, st = _conv_sample(x, state_conv[l], *conv_args(l))
        conv_sample.append(st)
        x = moe_block(x, l, False)
    ckv_s, kpe_s, ckvb_s, kpeb_s = _shared_kv(x.reshape(-1, d), wdkv, _row(s_g_kv), wkr, wkrs, cos_s, sin_s,
                                              tm=512, table_blocks=(bs * ts) // 512)
    cache_kpe_t = jnp.transpose(cache_kpe, (0, 2, 1))
    new_rows = LANES
    cnew =jnp.pad(ckvb_s.reshape(bs, ts, KV_LORA), ((0, 0), (0, new_rows - ts), (0, 0)))
    pnew = jnp.pad(kpeb_s.reshape(bs, ts, ROPE_PAD), ((0, 0), (0, new_rows - ts), (0, 0)))
    for l in range(N_A_LAYERS, DEPTH):
        j = l - N_A_LAYERS
        ql, qp = _q_sample(x, cos_s, sin_s, *q_args(j))
        o_lat = _attn_sample(page_table, ql.reshape(bs, N_HEADS * ts, KV_LORA),
                             qp.reshape(bs, N_HEADS * ts, ROPE_PAD), cnew, pnew, cache_ckv, cache_kpe_t)
        x = _o_sample(o_lat.reshape(bs, N_HEADS, ts, KV_LORA), x, wuv, wo[j],
                      _row(ln_mix_g[l]), _row(ln_mix_b[l]))
        x = moe_block(x, l, False)
    y_sample = x

    return (y_prompt, y_sample, jnp.stack(conv_prompt), jnp.stack(conv_sample),
            ckv_p.reshape(bp, sp, KV_LORA), kpe_p.reshape(bp, sp, QK_ROPE),
            ckv_s.reshape(bs, ts, KV_LORA), kpe_s.reshape(bs, ts, QK_ROPE))
```

```python
import functools

import jax
import jax.numpy as jnp
from jax import lax
from jax.experimental import pallas as pl
from jax.experimental.pallas import tpu as pltpu

D_MODEL = 1024
DEPTH = 4
N_A_LAYERS = DEPTH // 2
CONV_WIDTH = 31
CONV_STATE = CONV_WIDTH - 1
N_HEADS = 8
QK_NOPE = 128
QK_ROPE = 64
V_HEAD = 128
KV_LORA = 256
Q_LORA = 512
ROPE_BASE = 10000.0
SCORE_SCALE = (QK_NOPE + QK_ROPE) ** -0.5
N_EXPERTS = 16
N_GROUPS = 4
EXPERTS_PER_GROUP = N_EXPERTS // N_GROUPS
D_EXPERT = 512
ALPHA = (2 * DEPTH) ** 0.25
LN_EPS = 1e-5
RMS_EPS = 1e-6

LANES = 128
SUBLANES = 8
VMEM_LIMIT_BYTES = 56 * 1024 * 1024

ROPE_PAD = LANES
HIST_ROWS = 32
HIST_SKIP = HIST_ROWS - CONV_STATE

MEMBER_PAIRS = ((0, 1), (0, 2), (0, 3), (1, 2), (1, 3), (2, 3))
N_CLASSES = N_GROUPS * len(MEMBER_PAIRS)
CLASS_ROWS = 32
GATE_PAD = LANES
MOE_TILE = 512

BF16 = jnp.bfloat16
F32 = jnp.float32
NEG_INF = float("-inf")


def _cparams(*sem):
    return pltpu.CompilerParams(dimension_semantics=sem, vmem_limit_bytes=VMEM_LIMIT_BYTES)


def _dot(a, b):
    return jnp.dot(a, b, preferred_element_type=F32)


def _dot_nt(a, b):
    return lax.dot_general(a, b, (((1,), (1,)), ((), ())), preferred_element_type=F32)


def _layer_norm(v, g, b):
    mu = jnp.mean(v, axis=-1, keepdims=True)
    d = v - mu
    var = jnp.mean(d * d, axis=-1, keepdims=True)
    return d * lax.rsqrt(var + LN_EPS) * g + b


def _rms_norm(v, g):
    return v * lax.rsqrt(jnp.mean(v * v, axis=-1, keepdims=True) + RMS_EPS) * g


def _sigmoid(v):
    return 1.0 / (1.0 + jnp.exp(-v))


def _full(shape):
    n = len(shape)
    return pl.BlockSpec(shape, lambda *_: (0,) * n)


def _conv_tail(conv, x, gcn, bcn, wpw2, bpw2, lg, lb):
    z = _layer_norm(conv, gcn, bcn)
    z = z * _sigmoid(z)
    mix = _dot(z.astype(BF16), wpw2) + bpw2
    return _layer_norm(ALPHA * x + mix, lg, lb)


def _conv_prompt_kernel(x_ref, wpw1_ref, bpw1_ref, w8_ref, bdw_ref, gcn_ref, bcn_ref, wpw2_ref, bpw2_ref,
                        lg_ref, lb_ref, y_ref, state_ref, ext_ref, conv_ref, *, ts, parts, row_chunk, lane_chunk):
    s = pl.program_id(1)
    base = ext_ref.at[0]

    @pl.when(s == 0)
    def _():
        base[0:HIST_ROWS, :] = jnp.zeros((HIST_ROWS, D_MODEL), F32)

    @pl.when(s > 0)
    def _():
        base[0:HIST_ROWS, :] = base[ts:ts + HIST_ROWS, :]

    groups = row_chunk // SUBLANES
    part = ts // parts

    def glu(r0):
        h = _dot(x_ref[r0:r0 + part, :].astype(BF16), wpw1_ref[...]) + bpw1_ref[...]
        base[HIST_ROWS + r0:HIST_ROWS + r0 + part, :] = h[:, :D_MODEL] * _sigmoid(h[:, D_MODEL:])

    def conv_rows(r0):
        span = part + HIST_ROWS - SUBLANES
        for j in range(1, SUBLANES):
            ext_ref[j, r0:r0 + span, :] = base[r0 + j:r0 + j + span, :]
        for c0 in range(r0, r0 + part, row_chunk):
            for l0 in range(0, D_MODEL, lane_chunk):
                acc = jnp.broadcast_to(bdw_ref[:, l0:l0 + lane_chunk][None], (groups, SUBLANES, lane_chunk))
                for k in range(CONV_WIDTH):
                    off = HIST_SKIP + k
                    j, a = off % SUBLANES, off // SUBLANES
                    start = c0 + SUBLANES * a
                    blk = ext_ref[j, start:start + row_chunk, l0:l0 + lane_chunk]
                    acc = acc + blk.reshape(groups, SUBLANES, lane_chunk) * w8_ref[k, :, l0:l0 + lane_chunk][None]
                conv_ref[c0:c0 + row_chunk, l0:l0 + lane_chunk] = acc.reshape(row_chunk, lane_chunk)
        y_ref[r0:r0 + part, :] = _conv_tail(conv_ref[r0:r0 + part, :], x_ref[r0:r0 + part, :], gcn_ref[...],
                                            bcn_ref[...], wpw2_ref[...], bpw2_ref[...], lg_ref[...], lb_ref[...])

    for p in range(parts):
        glu(p * part)
    for p in range(parts):
        conv_rows(p * part)

    @pl.when(s == pl.num_programs(1) - 1)
    def _():
        state_ref[...] = base[ts + HIST_SKIP:ts + HIST_ROWS, :]


def _conv_prompt(x, wpw1, bpw1, w8, bdw, gcn, bcn, wpw2, bpw2, lg, lb, *, ts=256):
    b, s, d = x.shape
    kern = functools.partial(_conv_prompt_kernel, ts=ts, parts=2, row_chunk=32, lane_chunk=256)
    return pl.pallas_call(
        kern,
        grid=(b, s // ts),
        in_specs=[
            pl.BlockSpec((None, ts, d), lambda i, j: (i, j, 0)),
            _full(wpw1.shape), _full(bpw1.shape), _full(w8.shape), _full(bdw.shape), _full(gcn.shape),
            _full(bcn.shape), _full(wpw2.shape), _full(bpw2.shape), _full(lg.shape), _full(lb.shape),
        ],
        out_specs=[
            pl.BlockSpec((None, ts, d), lambda i, j: (i, j, 0)),
            pl.BlockSpec((None, CONV_STATE, d), lambda i, j: (i, 0, 0)),
        ],
        out_shape=[
            jax.ShapeDtypeStruct((b, s, d), F32),
            jax.ShapeDtypeStruct((b, CONV_STATE, d), F32),
        ],
        scratch_shapes=[
            pltpu.VMEM((SUBLANES, ts + HIST_ROWS, d), F32),
            pltpu.VMEM((ts, d), F32),
        ],
        compiler_params=_cparams("arbitrary", "arbitrary"),
        name="conv_prompt",
    )(x, wpw1, bpw1, w8, bdw, gcn, bcn, wpw2, bpw2, lg, lb)


def _conv_sample_kernel(x_ref, past_ref, wpw1_ref, bpw1_ref, w8_ref, bdw_ref, gcn_ref, bcn_ref, wpw2_ref,
                        bpw2_ref, lg_ref, lb_ref, y_ref, state_ref, ext_ref, conv_ref, *, bb, t, lane_chunk):
    x = x_ref[...].reshape(bb * t, D_MODEL)
    h = _dot(x.astype(BF16), wpw1_ref[...]) + bpw1_ref[...]
    u = h[:, :D_MODEL] * _sigmoid(h[:, D_MODEL:])
    ext_ref[:, HIST_SKIP:HIST_ROWS, :] = past_ref[...]
    ext_ref[:, HIST_ROWS:HIST_ROWS + t, :] = u.reshape(bb, t, D_MODEL)
    for l0 in range(0, D_MODEL, lane_chunk):
        acc = jnp.broadcast_to(bdw_ref[:, l0:l0 + lane_chunk][None], (bb, t, lane_chunk))
        for k in range(CONV_WIDTH):
            off = HIST_SKIP + k
            acc = acc + ext_ref[:, off:off + t, l0:l0 + lane_chunk] * w8_ref[k, :, l0:l0 + lane_chunk][None]
        conv_ref[:, :, l0:l0 + lane_chunk] = acc
    y = _conv_tail(conv_ref[...].reshape(bb * t, D_MODEL), x, gcn_ref[...], bcn_ref[...], wpw2_ref[...],
                   bpw2_ref[...], lg_ref[...], lb_ref[...])
    y_ref[...] = y.reshape(bb, t, D_MODEL)
    state_ref[...] = ext_ref[:, HIST_SKIP + t:HIST_ROWS + t, :]


def _conv_sample(x, past, wpw1, bpw1, w8, bdw, gcn, bcn, wpw2, bpw2, lg, lb, *, bb=16):
    b, t, d = x.shape
    assert t == SUBLANES
    kern = functools.partial(_conv_sample_kernel, bb=bb, t=t, lane_chunk=256)
    return pl.pallas_call(
        kern,
        grid=(b // bb,),
        in_specs=[
            pl.BlockSpec((bb, t, d), lambda i: (i, 0, 0)),
            pl.BlockSpec((bb, CONV_STATE, d), lambda i: (i, 0, 0)),
            _full(wpw1.shape), _full(bpw1.shape), _full(w8.shape), _full(bdw.shape), _full(gcn.shape),
            _full(bcn.shape), _full(wpw2.shape), _full(bpw2.shape), _full(lg.shape), _full(lb.shape),
        ],
        out_specs=[
            pl.BlockSpec((bb, t, d), lambda i: (i, 0, 0)),
            pl.BlockSpec((bb, CONV_STATE, d), lambda i: (i, 0, 0)),
        ],
        out_shape=[
            jax.ShapeDtypeStruct((b, t, d), F32),
            jax.ShapeDtypeStruct((b, CONV_STATE, d), F32),
        ],
        scratch_shapes=[
            pltpu.VMEM((bb, HIST_ROWS + t, d), F32),
            pltpu.VMEM((bb, t, d), F32),
        ],
        compiler_params=_cparams("arbitrary"),
        name="conv_sample",
    )(x, past, wpw1, bpw1, w8, bdw, gcn, bcn, wpw2, bpw2, lg, lb)


def _router_kernel(x_ref, rwt_ref, rb_ref, gates_ref, cls_ref, pair_ref, cnt_ref):
    x = x_ref[...]
    w = rwt_ref[...]
    x_hi = x.astype(BF16)
    x_lo = (x - x_hi.astype(F32)).astype(BF16)
    w_hi = w.astype(BF16)
    w_lo = (w - w_hi.astype(F32)).astype(BF16)
    by_hi = _dot_nt(jnp.concatenate([w_hi, w_lo], axis=0), x_hi)
    logits = by_hi[:N_EXPERTS] + by_hi[N_EXPERTS:] + _dot_nt(w_hi, x_lo)
    aff = _sigmoid(logits)
    sel = aff + rb_ref[...]
    g = N_GROUPS
    s = [sel[m * g:(m + 1) * g] for m in range(EXPERTS_PER_GROUP)]
    a = [aff[m * g:(m + 1) * g] for m in range(EXPERTS_PER_GROUP)]
    hi01, lo01 = jnp.maximum(s[0], s[1]), jnp.minimum(s[0], s[1])
    hi23, lo23 = jnp.maximum(s[2], s[3]), jnp.minimum(s[2], s[3])
    top1 = jnp.maximum(hi01, hi23)
    top2 = jnp.maximum(jnp.minimum(hi01, hi23), jnp.maximum(lo01, lo23))
    score = top1 + top2
    best = score[0:1]
    best_idx = jnp.zeros(best.shape, jnp.int32)
    for gi in range(1, g):
        better = score[gi:gi + 1] > best
        best = jnp.where(better, score[gi:gi + 1], best)
        best_idx = jnp.where(better, gi, best_idx)
    in_group = lax.broadcasted_iota(jnp.int32, score.shape, 0) == best_idx
    picked, chosen = [], []
    for m in range(EXPERTS_PER_GROUP):
        rank = jnp.zeros(score.shape, jnp.int32)
        for j in range(EXPERTS_PER_GROUP):
            if j == m:
                continue
            ahead = (s[j] >= s[m]) if j < m else (s[j] > s[m])
            rank = rank + ahead.astype(jnp.int32)
        keep = in_group & (rank < 2)
        picked.append(jnp.where(keep, a[m], 0.0))
        chosen.append(jnp.where(keep, 1.0, 0.0))
    p = [jnp.sum(v, axis=0, keepdims=True) for v in picked]
    on = [jnp.sum(v, axis=0, keepdims=True) > 0.5 for v in chosen]
    inv = 1.0 / (p[0] + p[1] + p[2] + p[3])
    for m in range(EXPERTS_PER_GROUP):
        gates_ref[m * g:(m + 1) * g, :] = picked[m] * inv
    lo = jnp.where(on[0], 0, jnp.where(on[1], 1, 2))
    hi = jnp.where(on[3], 3, jnp.where(on[2], 2, 1))
    p_lo = jnp.where(on[0], p[0], jnp.where(on[1], p[1], p[2]))
    p_hi = jnp.where(on[3], p[3], jnp.where(on[2], p[2], p[1]))
    pair = jnp.where(lo == 0, hi - 1, jnp.where(lo == 1, hi + 1, len(MEMBER_PAIRS) - 1))
    cls = best_idx * len(MEMBER_PAIRS) + pair
    cls_ref[...] = cls

    @pl.when(pl.program_id(0) == 0)
    def _():
        cnt_ref[...] = jnp.zeros(cnt_ref.shape, F32)

    onehot = lax.broadcasted_iota(jnp.int32, (CLASS_ROWS, cls.shape[1]), 0) == cls
    cnt_ref[...] += jnp.sum(jnp.where(onehot, 1.0, 0.0), axis=1, keepdims=True)
    pair_ref[...] = jnp.zeros(pair_ref.shape, F32)
    pair_ref[0:1, :] = p_lo * inv
    pair_ref[1:2, :] = p_hi * inv


def _router(x2d, rwt, rb, *, tm=1024):
    t, d = x2d.shape
    return pl.pallas_call(
        _router_kernel,
        grid=(t // tm,),
        in_specs=[pl.BlockSpec((tm, d), lambda i: (i, 0)), _full(rwt.shape), _full(rb.shape)],
        out_specs=[
            pl.BlockSpec((N_EXPERTS, tm), lambda i: (0, i)),
            pl.BlockSpec((1, tm), lambda i: (0, i)),
            pl.BlockSpec((SUBLANES, tm), lambda i: (0, i)),
            pl.BlockSpec((CLASS_ROWS, LANES), lambda i: (0, 0)),
        ],
        out_shape=[
            jax.ShapeDtypeStruct((N_EXPERTS, t), F32),
            jax.ShapeDtypeStruct((1, t), jnp.int32),
            jax.ShapeDtypeStruct((SUBLANES, t), F32),
            jax.ShapeDtypeStruct((CLASS_ROWS, LANES), F32),
        ],
        compiler_params=_cparams("arbitrary"),
        name="router",
    )(x2d, rwt, rb)


def _positions_kernel(cls_ref, cnt_ref, pos_ref, tcls_ref, nt_ref, start_ref, run_ref, *, tb, tm):
    i = pl.program_id(0)
    shift = tm.bit_length() - 1
    onehot = lax.broadcasted_iota(jnp.int32, (CLASS_ROWS, tb), 0) == cls_ref[...]
    ohf = jnp.where(onehot, 1.0, 0.0)

    @pl.when(i == 0)
    def _():
        cnt = cnt_ref[...].astype(jnp.int32)
        padded = (((cnt + (tm - 1)) >> shift) << shift).astype(F32)
        r = lax.broadcasted_iota(jnp.int32, (CLASS_ROWS, CLASS_ROWS), 0)
        c = lax.broadcasted_iota(jnp.int32, (CLASS_ROWS, CLASS_ROWS), 1)
        start = jnp.dot(jnp.where(c < r, 1.0, 0.0), padded, preferred_element_type=F32,
                        precision=lax.Precision.HIGHEST)
        start_ref[...] = start
        run_ref[...] = jnp.zeros(run_ref.shape, F32)
        tile_start = (lax.broadcasted_iota(jnp.int32, (CLASS_ROWS, LANES), 1) << shift).astype(F32)
        real = lax.broadcasted_iota(jnp.int32, (CLASS_ROWS, LANES), 0) < N_CLASSES
        below = jnp.where(real, jnp.where(start <= tile_start, 1.0, 0.0), 0.0)
        tcls_ref[...] = jnp.sum(below, axis=0, keepdims=True).astype(jnp.int32) - 1
        nt_ref[...] = jnp.sum(padded, axis=0, keepdims=True).astype(jnp.int32) >> shift

    rr = lax.broadcasted_iota(jnp.int32, (tb, tb), 0)
    cc = lax.broadcasted_iota(jnp.int32, (tb, tb), 1)
    upper = jnp.where(rr <= cc, 1.0, 0.0).astype(BF16)
    cum = _dot(ohf.astype(BF16), upper)
    base = run_ref[:, 0:1] + start_ref[:, 0:1] - 1.0
    pos_ref[...] = jnp.sum(ohf * (cum + base), axis=0, keepdims=True).astype(jnp.int32)
    run_ref[...] += cum[:, tb - 1:tb]


def _positions(cls, cnt, *, tb=512, tm=MOE_TILE):
    t = cls.shape[1]
    assert tm & (tm - 1) == 0 and (t + N_CLASSES * tm) // tm <= LANES
    kern = functools.partial(_positions_kernel, tb=tb, tm=tm)
    return pl.pallas_call(
        kern,
        grid=(t // tb,),
        in_specs=[pl.BlockSpec((1, tb), lambda i: (0, i)), _full(cnt.shape)],
        out_specs=[
            pl.BlockSpec((1, tb), lambda i: (0, i)),
            pl.BlockSpec((1, LANES), lambda i: (0, 0)),
            pl.BlockSpec((1, LANES), lambda i: (0, 0)),
        ],
        out_shape=[
            jax.ShapeDtypeStruct((1, t), jnp.int32),
            jax.ShapeDtypeStruct((1, LANES), jnp.int32),
            jax.ShapeDtypeStruct((1, LANES), jnp.int32),
        ],
        scratch_shapes=[pltpu.VMEM((CLASS_ROWS, LANES), F32)] * 2,
        compiler_params=_cparams("arbitrary"),
        name="positions",
    )(cls, cnt)


def _staged_row(ref, c, u):
    return ref.at[c, pl.ds(u, 1), :]


def _hbm_row(ref, r):
    return ref.at[r >> (SUBLANES.bit_length() - 1), pl.ds(r & (SUBLANES - 1), 1), :]


def _row_scatter_kernel(pos_ref, x_ref, pair_ref, init_hbm, dst_hbm, aug_ref, gcol_ref, sem, *, tb):
    del init_hbm
    groups = tb // SUBLANES
    aug_ref[:, :, :D_MODEL] = x_ref[...]
    gcol_ref[...] = jnp.zeros(gcol_ref.shape, F32)
    gcol_ref[0:SUBLANES, :] = pair_ref[...]
    aug_ref[:, :, D_MODEL:] = gcol_ref[...].T.reshape(groups, SUBLANES, GATE_PAD)

    def issue(c, carry):
        for u in range(SUBLANES):
            dst = _hbm_row(dst_hbm, pos_ref[0, c * SUBLANES + u])
            pltpu.make_async_copy(_staged_row(aug_ref, c, u), dst, sem).start(priority=u % 2)
        return carry

    lax.fori_loop(0, groups, issue, 0)
    pltpu.make_async_copy(aug_ref, dst_hbm.at[pl.ds(0, groups)], sem).wait()


def _row_scatter(pos3, x2d, pair, init):
    nb, _, tb = pos3.shape
    t, d = x2d.shape
    rows, da = init.shape
    kern = functools.partial(_row_scatter_kernel, tb=tb)
    out = pl.pallas_call(
        kern,
        grid=(nb,),
        in_specs=[
            pl.BlockSpec((None, 1, tb), lambda i: (i, 0, 0), memory_space=pltpu.SMEM),
            pl.BlockSpec((tb // SUBLANES, SUBLANES, d), lambda i: (i, 0, 0)),
            pl.BlockSpec((SUBLANES, tb), lambda i: (0, i)),
            pl.BlockSpec(memory_space=pl.ANY),
        ],
        out_specs=pl.BlockSpec(memory_space=pl.ANY),
        out_shape=jax.ShapeDtypeStruct((rows // SUBLANES, SUBLANES, da), init.dtype),
        scratch_shapes=[
            pltpu.VMEM((tb // SUBLANES, SUBLANES, da), F32),
            pltpu.VMEM((GATE_PAD, tb), F32),
            pltpu.SemaphoreType.DMA,
        ],
        input_output_aliases={3: 0},
        compiler_params=_cparams("arbitrary"),
        name="row_scatter",
    )(pos3, x2d.reshape(t // SUBLANES, SUBLANES, d), pair, init.reshape(rows // SUBLANES, SUBLANES, da))
    return out.reshape(rows, da)


def _row_gather_kernel(pos_ref, src_hbm, y_ref, sem, *, tb):
    groups = tb // SUBLANES

    def issue(c, carry):
        for u in range(SUBLANES):
            src = _hbm_row(src_hbm, pos_ref[0, c * SUBLANES + u])
            pltpu.make_async_copy(src, _staged_row(y_ref, c, u), sem).start(priority=u % 2)
        return carry

    lax.fori_loop(0, groups, issue, 0)
    pltpu.make_async_copy(src_hbm.at[pl.ds(0, groups)], y_ref, sem).wait()


def _row_gather(pos3, src):
    nb, _, tb = pos3.shape
    rows, d = src.shape
    kern = functools.partial(_row_gather_kernel, tb=tb)
    out = pl.pallas_call(
        kern,
        grid=(nb,),
        in_specs=[
            pl.BlockSpec((None, 1, tb), lambda i: (i, 0, 0), memory_space=pltpu.SMEM),
            pl.BlockSpec(memory_space=pl.ANY),
        ],
        out_specs=pl.BlockSpec((tb // SUBLANES, SUBLANES, d), lambda i: (i, 0, 0)),
        out_shape=jax.ShapeDtypeStruct((nb * tb // SUBLANES, SUBLANES, d), src.dtype),
        scratch_shapes=[pltpu.SemaphoreType.DMA],
        compiler_params=_cparams("arbitrary"),
        name="row_gather",
    )(pos3, src.reshape(rows // SUBLANES, SUBLANES, d))
    return out.reshape(nb * tb, d)


def _moe_sorted_kernel(ea_ref, eb_ref, nt_ref, xs_ref, w1a_ref, w3a_ref, w2a_ref, w1b_ref, w3b_ref, w2b_ref,
                       lg_ref, lb_ref, y_ref, *wb_refs):
    i = pl.program_id(0)
    used = i < nt_ref[0]
    prev = jnp.maximum(i - 1, 0)
    new_pair = (i == 0) | (ea_ref[i] != ea_ref[prev]) | (eb_ref[i] != eb_ref[prev])

    @pl.when(jnp.logical_not(used))
    def _():
        y_ref[...] = jnp.zeros(y_ref.shape, F32)

    @pl.when(used & new_pair)
    def _():
        for src, dst in zip((w1a_ref, w3a_ref, w2a_ref, w1b_ref, w3b_ref, w2b_ref), wb_refs):
            dst[...] = src[...].astype(BF16)

    @pl.when(used)
    def _():
        x = xs_ref[:, :D_MODEL]
        xb = x.astype(BF16)

        def ffn(w1_ref, w3_ref, w2_ref):
            h1 = _dot(xb, w1_ref[...])
            h3 = _dot(xb, w3_ref[...])
            return _dot((h1 * _sigmoid(h1) * h3).astype(BF16), w2_ref[...])

        out = xs_ref[:, D_MODEL:D_MODEL + 1] * ffn(*wb_refs[:3])
        out = out + xs_ref[:, D_MODEL + 1:D_MODEL + 2] * ffn(*wb_refs[3:])
        y_ref[...] = _layer_norm(ALPHA * x + out, lg_ref[...], lb_ref[...])


def _moe_sorted(ea, eb, nt, xs, layer, w1, w3, w2, lg, lb, *, tm=MOE_TILE):
    p, da = xs.shape
    _, _, d, f = w1.shape

    def row_block(i, ea, eb, nt):
        return (i, 0)

    def wa(i, ea, eb, nt):
        return (layer, ea[i], 0, 0)

    def wb(i, ea, eb, nt):
        return (layer, eb[i], 0, 0)

    grid_spec = pltpu.PrefetchScalarGridSpec(
        num_scalar_prefetch=3,
        grid=(p // tm,),
        in_specs=[
            pl.BlockSpec((tm, da), row_block),
            pl.BlockSpec((None, None, d, f), wa), pl.BlockSpec((None, None, d, f), wa),
            pl.BlockSpec((None, None, f, d), wa),
            pl.BlockSpec((None, None, d, f), wb), pl.BlockSpec((None, None, d, f), wb),
            pl.BlockSpec((None, None, f, d), wb),
            pl.BlockSpec(lg.shape, lambda i, ea, eb, nt: (0, 0)),
            pl.BlockSpec(lb.shape, lambda i, ea, eb, nt: (0, 0)),
        ],
        out_specs=pl.BlockSpec((tm, d), row_block),
        scratch_shapes=[pltpu.VMEM((d, f), BF16), pltpu.VMEM((d, f), BF16), pltpu.VMEM((f, d), BF16)] * 2,
    )
    return pl.pallas_call(
        _moe_sorted_kernel,
        grid_spec=grid_spec,
        out_shape=jax.ShapeDtypeStruct((p, d), F32),
        compiler_params=_cparams("arbitrary"),
        name="moe_sorted",
    )(ea, eb, nt, xs, w1, w3, w2, w1, w3, w2, lg, lb)


def _moe_kernel(x_ref, gates_ref, w1_ref, w3_ref, w2_ref, lg_ref, lb_ref, y_ref, xb_ref, acc_ref):
    e = pl.program_id(1)

    @pl.when(e == 0)
    def _():
        xb_ref[...] = x_ref[...].astype(BF16)
        acc_ref[...] = jnp.zeros(acc_ref.shape, F32)

    xb = xb_ref[...]
    h1 = _dot(xb, w1_ref[...].astype(BF16))
    h3 = _dot(xb, w3_ref[...].astype(BF16))
    h = (h1 * _sigmoid(h1) * h3).astype(BF16)
    lane = lax.broadcasted_iota(jnp.int32, gates_ref.shape, 1)
    gate = jnp.sum(jnp.where(lane == e, gates_ref[...], 0.0), axis=1, keepdims=True)
    acc_ref[...] += gate * _dot(h, w2_ref[...].astype(BF16))

    @pl.when(e == pl.num_programs(1) - 1)
    def _():
        y_ref[...] = _layer_norm(ALPHA * x_ref[...] + acc_ref[...], lg_ref[...], lb_ref[...])


def _moe(x2d, gates, layer, w1, w3, w2, lg, lb, *, tm=1024):
    t, d = x2d.shape
    _, e, _, f = w1.shape
    return pl.pallas_call(
        _moe_kernel,
        grid=(t // tm, e),
        in_specs=[
            pl.BlockSpec((tm, d), lambda i, j: (i, 0)),
            pl.BlockSpec((tm, e), lambda i, j: (i, 0)),
            pl.BlockSpec((None, None, d, f), lambda i, j: (layer, j, 0, 0)),
            pl.BlockSpec((None, None, d, f), lambda i, j: (layer, j, 0, 0)),
            pl.BlockSpec((None, None, f, d), lambda i, j: (layer, j, 0, 0)),
            _full(lg.shape), _full(lb.shape),
        ],
        out_specs=pl.BlockSpec((tm, d), lambda i, j: (i, 0)),
        out_shape=jax.ShapeDtypeStruct((t, d), F32),
        scratch_shapes=[pltpu.VMEM((tm, d), BF16), pltpu.VMEM((tm, d), F32)],
        compiler_params=_cparams("arbitrary", "arbitrary"),
        name="moe",
    )(x2d, gates, w1, w3, w2, lg, lb)


def _shared_kv_kernel(x_ref, wdkv_ref, gkv_ref, wkr_ref, wkrs_ref, cos_ref, sin_ref,
                      ckv_ref, kpe_ref, ckvb_ref, kpeb_ref):
    xb = x_ref[...].astype(BF16)
    ckv = _rms_norm(_dot(xb, wdkv_ref[...]), gkv_ref[...])
    kpe = _dot(xb, wkr_ref[...]) * cos_ref[...] + _dot(xb, wkrs_ref[...]) * sin_ref[...]
    ckv_ref[...] = ckv
    kpe_ref[...] = kpe[:, :QK_ROPE]
    ckvb_ref[...] = ckv.astype(BF16)
    kpeb_ref[...] = kpe.astype(BF16)


def _shared_kv(x2d, wdkv, gkv, wkr, wkrs, cos, sin, *, tm, table_blocks):
    t, d = x2d.shape
    tab = pl.BlockSpec((tm, ROPE_PAD), lambda i: (i % table_blocks, 0))
    return pl.pallas_call(
        _shared_kv_kernel,
        grid=(t // tm,),
        in_specs=[pl.BlockSpec((tm, d), lambda i: (i, 0)), _full(wdkv.shape), _full(gkv.shape),
                  _full(wkr.shape), _full(wkrs.shape), tab, tab],
        out_specs=[
            pl.BlockSpec((tm, KV_LORA), lambda i: (i, 0)),
            pl.BlockSpec((tm, QK_ROPE), lambda i: (i, 0)),
            pl.BlockSpec((tm, KV_LORA), lambda i: (i, 0)),
            pl.BlockSpec((tm, ROPE_PAD), lambda i: (i, 0)),
        ],
        out_shape=[
            jax.ShapeDtypeStruct((t, KV_LORA), F32),
            jax.ShapeDtypeStruct((t, QK_ROPE), F32),
            jax.ShapeDtypeStruct((t, KV_LORA), BF16),
            jax.ShapeDtypeStruct((t, ROPE_PAD), BF16),
        ],
        compiler_params=_cparams("arbitrary"),
        name="shared_kv",
    )(x2d, wdkv, gkv, wkr, wkrs, cos, sin)


def _project_queries(x, cos, sin, wdq, gq, wuqn, wuqp, wuqps, wukt_ref, store):
    cq = _rms_norm(_dot(x.astype(BF16), wdq), gq).astype(BF16)
    qn = _dot(cq, wuqn)
    qp = _dot(cq, wuqp)
    qps = _dot(cq, wuqps)
    cos, sin = cos * SCORE_SCALE, sin * SCORE_SCALE
    for h in range(N_HEADS):
        lat = _dot(qn[:, h * QK_NOPE:(h + 1) * QK_NOPE].astype(BF16), wukt_ref[h]) * SCORE_SCALE
        sl = slice(h * ROPE_PAD, (h + 1) * ROPE_PAD)
        pe = qp[:, sl] * cos + qps[:, sl] * sin
        store(h, lat.astype(BF16), pe.astype(BF16))


def _project_output(o_heads, x, wuv_ref, wo, lg, lb):
    o = jnp.concatenate([_dot(o_heads[h].astype(BF16), wuv_ref[h]) for h in range(N_HEADS)], axis=-1)
    mix = _dot(o.astype(BF16), wo)
    return _layer_norm(ALPHA * x + mix, lg, lb)


def _attn_prompt_kernel(x_ref, ckv_ref, kpe_ref, cos_ref, sin_ref, wdq_ref, gq_ref, wuqn_ref, wuqp_ref,
                        wuqps_ref, wukt_ref, wuv_ref, wo_ref, lg_ref, lb_ref, y_ref,
                        ql_ref, qp_ref, m_ref, l_ref, acc_ref, *, tq, tk, group_heads):
    i = pl.program_id(1)
    x = x_ref[...]

    def store(h, lat, pe):
        ql_ref[h * tq:(h + 1) * tq, :] = lat
        qp_ref[h * tq:(h + 1) * tq, :] = pe

    _project_queries(x, cos_ref[...], sin_ref[...], wdq_ref[...], gq_ref[...], wuqn_ref[...], wuqp_ref[...],
                     wuqps_ref[...], wukt_ref, store)

    rows = N_HEADS * tq
    m_ref[...] = jnp.full((rows, 1), NEG_INF, F32)
    l_ref[...] = jnp.zeros((rows, 1), F32)
    acc_ref[...] = jnp.zeros((rows, KV_LORA), F32)

    group = group_heads * tq

    def block(kb, masked):
        k0 = pl.multiple_of(kb * tk, tk)
        kc = ckv_ref[pl.ds(k0, tk), :]
        kp = kpe_ref[pl.ds(k0, tk), :]
        for r0 in range(0, rows, group):
            rs = slice(r0, r0 + group)
            s = _dot_nt(ql_ref[rs, :], kc) + _dot_nt(qp_ref[rs, :], kp)
            if masked:
                r = lax.broadcasted_iota(jnp.int32, (group, tk), 0) & (tq - 1)
                c = lax.broadcasted_iota(jnp.int32, (group, tk), 1)
                s = jnp.where(k0 + c <= i * tq + r, s, NEG_INF)
            m_old = m_ref[rs, :]
            m_new = jnp.maximum(m_old, jnp.max(s, axis=-1, keepdims=True))
            p = jnp.exp(s - m_new)
            scale = jnp.exp(m_old - m_new)
            l_ref[rs, :] = scale * l_ref[rs, :] + jnp.sum(p, axis=-1, keepdims=True)
            acc_ref[rs, :] = scale * acc_ref[rs, :] + _dot(p.astype(BF16), kc)
            m_ref[rs, :] = m_new

    n_full = (i * tq) // tk

    def body(kb, carry):
        block(kb, False)
        return carry

    lax.fori_loop(0, n_full, body, 0)
    block(n_full, True)

    inv = 1.0 / l_ref[...]
    o_heads = [acc_ref[h * tq:(h + 1) * tq, :] * inv[h * tq:(h + 1) * tq] for h in range(N_HEADS)]
    y_ref[...] = _project_output(o_heads, x, wuv_ref, wo_ref[...], lg_ref[...], lb_ref[...])


def _attn_prompt(x, ckvb, kpeb, cos, sin, wdq, gq, wuqn, wuqp, wuqps, wukt, wuv, wo, lg, lb, *, tq=256, tk=512,
                 group_heads=2):
    b, s, d = x.shape
    assert tk % tq == 0 and s % tk == 0 and tq & (tq - 1) == 0 and N_HEADS % group_heads == 0
    kern = functools.partial(_attn_prompt_kernel, tq=tq, tk=tk, group_heads=group_heads)
    rows = N_HEADS * tq
    return pl.pallas_call(
        kern,
        grid=(b, s // tq),
        in_specs=[
            pl.BlockSpec((None, tq, d), lambda i, j: (i, j, 0)),
            pl.BlockSpec((None, s, KV_LORA), lambda i, j: (i, 0, 0)),
            pl.BlockSpec((None, s, ROPE_PAD), lambda i, j: (i, 0, 0)),
            pl.BlockSpec((tq, ROPE_PAD), lambda i, j: (j, 0)),
            pl.BlockSpec((tq, ROPE_PAD), lambda i, j: (j, 0)),
            _full(wdq.shape), _full(gq.shape), _full(wuqn.shape), _full(wuqp.shape), _full(wuqps.shape),
            _full(wukt.shape), _full(wuv.shape), _full(wo.shape), _full(lg.shape), _full(lb.shape),
        ],
        out_specs=pl.BlockSpec((None, tq, d), lambda i, j: (i, j, 0)),
        out_shape=jax.ShapeDtypeStruct((b, s, d), F32),
        scratch_shapes=[
            pltpu.VMEM((rows, KV_LORA), BF16),
            pltpu.VMEM((rows, ROPE_PAD), BF16),
            pltpu.VMEM((rows, 1), F32),
            pltpu.VMEM((rows, 1), F32),
            pltpu.VMEM((rows, KV_LORA), F32),
        ],
        compiler_params=_cparams("arbitrary", "arbitrary"),
        name="attn_prompt",
    )(x, ckvb, kpeb, cos, sin, wdq, gq, wuqn, wuqp, wuqps, wukt, wuv, wo, lg, lb)


def _q_sample_kernel(x_ref, cos_ref, sin_ref, wdq_ref, gq_ref, wuqn_ref, wuqp_ref, wuqps_ref, wukt_ref,
                     ql_ref, qp_ref, *, bb, t):
    x = x_ref[...].reshape(bb * t, D_MODEL)

    def store(h, lat, pe):
        ql_ref[:, h] = lat.reshape(bb, t, KV_LORA)
        qp_ref[:, h] = pe.reshape(bb, t, ROPE_PAD)

    _project_queries(x, cos_ref[...], sin_ref[...], wdq_ref[...], gq_ref[...], wuqn_ref[...], wuqp_ref[...],
                     wuqps_ref[...], wukt_ref, store)


def _q_sample(x, cos, sin, wdq, gq, wuqn, wuqp, wuqps, wukt, *, bb=32):
    b, t, d = x.shape
    kern = functools.partial(_q_sample_kernel, bb=bb, t=t)
    return pl.pallas_call(
        kern,
        grid=(b // bb,),
        in_specs=[
            pl.BlockSpec((bb, t, d), lambda i: (i, 0, 0)),
            pl.BlockSpec((bb * t, ROPE_PAD), lambda i: (i, 0)),
            pl.BlockSpec((bb * t, ROPE_PAD), lambda i: (i, 0)),
            _full(wdq.shape), _full(gq.shape), _full(wuqn.shape), _full(wuqp.shape), _full(wuqps.shape),
            _full(wukt.shape),
        ],
        out_specs=[
            pl.BlockSpec((bb, N_HEADS, t, KV_LORA), lambda i: (i, 0, 0, 0)),
            pl.BlockSpec((bb, N_HEADS, t, ROPE_PAD), lambda i: (i, 0, 0, 0)),
        ],
        out_shape=[
            jax.ShapeDtypeStruct((b, N_HEADS, t, KV_LORA), BF16),
            jax.ShapeDtypeStruct((b, N_HEADS, t, ROPE_PAD), BF16),
        ],
        compiler_params=_cparams("arbitrary"),
        name="q_sample",
    )(x, cos, sin, wdq, gq, wuqn, wuqp, wuqps, wukt)


def _attn_sample_kernel(pt_ref, ql_ref, qp_ref, cnew_ref, pnew_ref, ckv_hbm, kpe_hbm, o_ref,
                        cbuf_ref, pbuf_ref, kc_ref, kp_ref, sem, *, pages, page, t):
    b = pl.program_id(0)
    nb = pl.num_programs(0)
    slot = b % 2
    rows = ql_ref.shape[0]

    def fetch(seq, into):
        for p in range(pages):
            src = pt_ref[seq * pages + p]
            pltpu.make_async_copy(ckv_hbm.at[src], cbuf_ref.at[into, p], sem.at[0, into]).start()
            pltpu.make_async_copy(kpe_hbm.at[src], pbuf_ref.at[into, p], sem.at[1, into]).start()

    @pl.when(b == 0)
    def _():
        fetch(0, 0)

    @pl.when(b + 1 < nb)
    def _():
        fetch(b + 1, 1 - slot)

    pltpu.make_async_copy(ckv_hbm.at[pl.ds(0, pages)], cbuf_ref.at[slot], sem.at[0, slot]).wait()
    pltpu.make_async_copy(kpe_hbm.at[pl.ds(0, pages)], pbuf_ref.at[slot], sem.at[1, slot]).wait()

    for p in range(pages):
        kc_ref[p * page:(p + 1) * page, :] = cbuf_ref[slot, p].astype(BF16)
        kp_ref[:, p * page:(p + 1) * page] = pbuf_ref[slot, p].astype(BF16)

    ql = ql_ref[...]
    qp = qp_ref[...]
    kc = kc_ref[...]
    cn = cnew_ref[...]
    s_past = _dot_nt(ql, kc) + _dot(qp[:, :QK_ROPE], kp_ref[...])
    s_new = _dot_nt(ql, cn) + _dot_nt(qp, pnew_ref[...])
    n = cn.shape[0]
    qpos = lax.broadcasted_iota(jnp.int32, (rows, n), 0) & (t - 1)
    kpos = lax.broadcasted_iota(jnp.int32, (rows, n), 1)
    s_new = jnp.where(kpos <= qpos, s_new, NEG_INF)
    m = jnp.maximum(jnp.max(s_past, axis=-1, keepdims=True), jnp.max(s_new, axis=-1, keepdims=True))
    p_past = jnp.exp(s_past - m)
    p_new = jnp.exp(s_new - m)
    l = jnp.sum(p_past, axis=-1, keepdims=True) + jnp.sum(p_new, axis=-1, keepdims=True)
    o_ref[...] = (_dot(p_past.astype(BF16), kc) + _dot(p_new.astype(BF16), cn)) * (1.0 / l)


def _attn_sample(page_table, ql, qp, cnew, pnew, cache_ckv, cache_kpe_t):
    b, rows, _ = ql.shape
    pages = page_table.shape[1]
    page = cache_ckv.shape[1]
    t = rows // N_HEADS
    n_new = cnew.shape[1]
    kern = functools.partial(_attn_sample_kernel, pages=pages, page=page, t=t)
    grid_spec = pltpu.PrefetchScalarGridSpec(
        num_scalar_prefetch=1,
        grid=(b,),
        in_specs=[
            pl.BlockSpec((None, rows, KV_LORA), lambda i, pt: (i, 0, 0)),
            pl.BlockSpec((None, rows, ROPE_PAD), lambda i, pt: (i, 0, 0)),
            pl.BlockSpec((None, n_new, KV_LORA), lambda i, pt: (i, 0, 0)),
            pl.BlockSpec((None, n_new, ROPE_PAD), lambda i, pt: (i, 0, 0)),
            pl.BlockSpec(memory_space=pl.ANY),
            pl.BlockSpec(memory_space=pl.ANY),
        ],
        out_specs=pl.BlockSpec((None, rows, KV_LORA), lambda i, pt: (i, 0, 0)),
        scratch_shapes=[
            pltpu.VMEM((2, pages, page, KV_LORA), F32),
            pltpu.VMEM((2, pages, QK_ROPE, page), F32),
            pltpu.VMEM((pages * page, KV_LORA), BF16),
            pltpu.VMEM((QK_ROPE, pages * page), BF16),
            pltpu.SemaphoreType.DMA((2, 2)),
        ],
    )
    return pl.pallas_call(
        kern,
        grid_spec=grid_spec,
        out_shape=jax.ShapeDtypeStruct((b, rows, KV_LORA), F32),
        compiler_params=_cparams("arbitrary"),
        name="attn_sample",
    )(page_table.reshape(-1), ql, qp, cnew, pnew, cache_ckv, cache_kpe_t)


def _o_sample_kernel(o_ref, x_ref, wuv_ref, wo_ref, lg_ref, lb_ref, y_ref, *, bb, t):
    x = x_ref[...].reshape(bb * t, D_MODEL)
    o_heads = [o_ref[:, h].reshape(bb * t, KV_LORA) for h in range(N_HEADS)]
    y_ref[...] = _project_output(o_heads, x, wuv_ref, wo_ref[...], lg_ref[...], lb_ref[...]).reshape(bb, t, D_MODEL)


def _o_sample(o_lat, x, wuv, wo, lg, lb, *, bb=32):
    b, t, d = x.shape
    kern = functools.partial(_o_sample_kernel, bb=bb, t=t)
    return pl.pallas_call(
        kern,
        grid=(b // bb,),
        in_specs=[
            pl.BlockSpec((bb, N_HEADS, t, KV_LORA), lambda i: (i, 0, 0, 0)),
            pl.BlockSpec((bb, t, d), lambda i: (i, 0, 0)),
            _full(wuv.shape), _full(wo.shape), _full(lg.shape), _full(lb.shape),
        ],
        out_specs=pl.BlockSpec((bb, t, d), lambda i: (i, 0, 0)),
        out_shape=jax.ShapeDtypeStruct((b, t, d), F32),
        compiler_params=_cparams("arbitrary"),
        name="o_sample",
    )(o_lat, x, wuv, wo, lg, lb)


def _rope_tables(pos):
    half = QK_ROPE // 2
    inv_freq = ROPE_BASE ** (-jnp.arange(half, dtype=F32) / half)
    ang = pos.astype(F32)[:, None] * inv_freq[None, :]
    cos, sin = jnp.cos(ang), jnp.sin(ang)
    pad = jnp.zeros((pos.shape[0], ROPE_PAD - QK_ROPE), F32)
    return (jnp.concatenate([cos, cos, pad], axis=-1), jnp.concatenate([-sin, sin, pad], axis=-1))


def _swap_halves(w):
    half = QK_ROPE // 2
    return jnp.concatenate([w[..., half:], w[..., :half]], axis=-1)


def _pad_rope(w):
    return jnp.pad(w, [(0, 0)] * (w.ndim - 1) + [(0, ROPE_PAD - QK_ROPE)])


def _row(v):
    return v.reshape(1, -1)


def kernel(x_prompt, x_sample, state_conv, cache_ckv, cache_kpe, page_table, a_w_pw1, a_b_pw1, a_w_dw, a_b_dw, a_g_cn, a_b_cn, a_w_pw2, a_b_pw2, ln_mix_g, ln_mix_b, ln_ffn_g, ln_ffn_b, b_w_dq, b_g_q, b_w_uq, b_w_o, s_w_dkv, s_g_kv, s_w_kr, s_w_uk, s_w_uv, r_w, r_b, e_w1, e_w3, e_w2):
    bp, sp, d = x_prompt.shape
    bs, ts, _ = x_sample.shape
    past_len = page_table.shape[1] * cache_ckv.shape[1]

    a_w_pw1b, a_w_pw2b = a_w_pw1.astype(BF16), a_w_pw2.astype(BF16)
    w8 = jnp.broadcast_to(a_w_dw[:, :, None, :], (N_A_LAYERS, CONV_WIDTH, SUBLANES, d))
    perm = jnp.arange(N_EXPERTS).reshape(N_GROUPS, EXPERTS_PER_GROUP).T.reshape(-1)
    rwt = r_w.T[perm]
    rbp = r_b[perm].reshape(N_EXPERTS, 1)
    wdkv = s_w_dkv.astype(BF16)
    wkr = _pad_rope(s_w_kr).astype(BF16)
    wkrs = _pad_rope(_swap_halves(s_w_kr)).astype(BF16)
    wukt = jnp.transpose(s_w_uk, (1, 2, 0)).astype(BF16)
    wuv = jnp.transpose(s_w_uv, (1, 0, 2)).astype(BF16)
    wdq = b_w_dq.astype(BF16)
    uq_pe = b_w_uq[..., QK_NOPE:]
    wuqn = b_w_uq[..., :QK_NOPE].reshape(-1, Q_LORA, N_HEADS * QK_NOPE).astype(BF16)
    wuqp = _pad_rope(uq_pe).reshape(-1, Q_LORA, N_HEADS * ROPE_PAD).astype(BF16)
    wuqps = _pad_rope(_swap_halves(uq_pe)).reshape(-1, Q_LORA, N_HEADS * ROPE_PAD).astype(BF16)
    wo = b_w_o.astype(BF16)

    cos_p, sin_p = _rope_tables(jnp.arange(sp))
    cos_s1, sin_s1 = _rope_tables(past_len + jnp.arange(ts))
    cos_s, sin_s = jnp.tile(cos_s1, (bs, 1)), jnp.tile(sin_s1, (bs, 1))

    cls_lo = jnp.array([g * EXPERTS_PER_GROUP + lo for g in range(N_GROUPS) for lo, _ in MEMBER_PAIRS], jnp.int32)
    cls_hi = jnp.array([g * EXPERTS_PER_GROUP + hi for g in range(N_GROUPS) for _, hi in MEMBER_PAIRS], jnp.int32)

    def moe_block(x, l, routed):
        x2d = x.reshape(-1, d)
        t = x2d.shape[0]
        ffn = (l, e_w1, e_w3, e_w2, _row(ln_ffn_g[l]), _row(ln_ffn_b[l]))
        gates_t, cls, pair, cnt = _router(x2d, rwt, rbp)
        if not routed:
            gates = gates_t.reshape(EXPERTS_PER_GROUP, N_GROUPS, -1).transpose(2, 1, 0).reshape(-1, N_EXPERTS)
            return _moe(x2d, gates, *ffn).reshape(x.shape)
        pos, tile_cls, n_tiles = _positions(cls, cnt)
        rows = t + N_CLASSES * MOE_TILE
        nt = n_tiles[0, :1]
        tile = jnp.arange(rows // MOE_TILE)
        tc = jnp.clip(tile_cls[0, jnp.minimum(tile, nt[0] - 1)], 0, N_CLASSES - 1)
        pos3 = pos.reshape(-1, 1, 2048)
        xs = _row_scatter(pos3, x2d, pair, jnp.zeros((rows, d + GATE_PAD), F32))
        ys = _moe_sorted(cls_lo[tc], cls_hi[tc], nt, xs, *ffn)
        return _row_gather(pos3, ys).reshape(x.shape)

    def conv_args(l):
        return (a_w_pw1b[l], _row(a_b_pw1[l]), w8[l], _row(a_b_dw[l]), _row(a_g_cn[l]), _row(a_b_cn[l]),
                a_w_pw2b[l], _row(a_b_pw2[l]), _row(ln_mix_g[l]), _row(ln_mix_b[l]))

    def q_args(j):
        return (wdq[j], _row(b_g_q[j]), wuqn[j], wuqp[j], wuqps[j], wukt)

    x = x_prompt
    conv_prompt = []
    for l in range(N_A_LAYERS):
        x, st = _conv_prompt(x, *conv_args(l))
        conv_prompt.append(st)
        x = moe_block(x, l, True)
    ckv_p, kpe_p, ckvb_p, kpeb_p = _shared_kv(x.reshape(-1, d), wdkv, _row(s_g_kv), wkr, wkrs, cos_p, sin_p,
                                              tm=512, table_blocks=sp // 512)
    ckvb_p3, kpeb_p3 = ckvb_p.reshape(bp, sp, KV_LORA), kpeb_p.reshape(bp, sp, ROPE_PAD)
    for l in range(N_A_LAYERS, DEPTH):
        j = l - N_A_LAYERS
        x = _attn_prompt(x, ckvb_p3, kpeb_p3, cos_p, sin_p, *q_args(j), wuv, wo[j],
                         _row(ln_mix_g[l]), _row(ln_mix_b[l]))
        x = moe_block(x, l, True)
    y_prompt = x

    x = x_sample
    conv_sample = []
    for l in range(N_A_LAYERS):
        x, st = _conv_sample(x, state_conv[l], *conv_args(l))
        conv_sample.append(st)
        x = moe_block(x, l, False)
    ckv_s, kpe_s, ckvb_s, kpeb_s = _shared_kv(x.reshape(-1, d), wdkv, _row(s_g_kv), wkr, wkrs, cos_s, sin_s,
                                              tm=512, table_blocks=(bs * ts) // 512)
    cache_kpe_t = jnp.transpose(cache_kpe, (0, 2, 1))
    new_rows = LANES
    cnew =jnp.pad(ckvb_s.reshape(bs, ts, KV_LORA), ((0, 0), (0, new_rows - ts), (0, 0)))
    pnew = jnp.pad(kpeb_s.reshape(bs, ts, ROPE_PAD), ((0, 0), (0, new_rows - ts), (0, 0)))
    for l in range(N_A_LAYERS, DEPTH):
        j = l - N_A_LAYERS
        ql, qp = _q_sample(x, cos_s, sin_s, *q_args(j))
        o_lat = _attn_sample(page_table, ql.reshape(bs, N_HEADS * ts, KV_LORA),
                             qp.reshape(bs, N_HEADS * ts, ROPE_PAD), cnew, pnew, cache_ckv, cache_kpe_t)
        x = _o_sample(o_lat.reshape(bs, N_HEADS, ts, KV_LORA), x, wuv, wo[j],
                      _row(ln_mix_g[l]), _row(ln_mix_b[l]))
        x = moe_block(x, l, False)
    y_sample = x

    return (y_prompt, y_sample, jnp.stack(conv_prompt), jnp.stack(conv_sample),
            ckv_p.reshape(bp, sp, KV_LORA), kpe_p.reshape(bp, sp, QK_ROPE),
            ckv_s.reshape(bs, ts, KV_LORA), kpe_s.reshape(bs, ts, QK_ROPE))
```

```python
import functools

import jax
import jax.numpy as jnp
from jax import lax
from jax.experimental import pallas as pl
from jax.experimental.pallas import tpu as pltpu

D_MODEL = 1024
DEPTH = 4
N_A_LAYERS = DEPTH // 2
CONV_WIDTH = 31
CONV_STATE = CONV_WIDTH - 1
N_HEADS = 8
QK_NOPE = 128
QK_ROPE = 64
V_HEAD = 128
KV_LORA = 256
Q_LORA = 512
ROPE_BASE = 10000.0
SCORE_SCALE = (QK_NOPE + QK_ROPE) ** -0.5
N_EXPERTS = 16
N_GROUPS = 4
EXPERTS_PER_GROUP = N_EXPERTS // N_GROUPS
D_EXPERT = 512
ALPHA = (2 * DEPTH) ** 0.25
LN_EPS = 1e-5
RMS_EPS = 1e-6

LANES = 128
SUBLANES = 8
VMEM_LIMIT_BYTES = 56 * 1024 * 1024

ROPE_PAD = LANES
HIST_ROWS = 32
HIST_SKIP = HIST_ROWS - CONV_STATE

MEMBER_PAIRS = ((0, 1), (0, 2), (0, 3), (1, 2), (1, 3), (2, 3))
N_CLASSES = N_GROUPS * len(MEMBER_PAIRS)
CLASS_ROWS = 32
GATE_PAD = LANES
MOE_TILE = 512

BF16 = jnp.bfloat16
F32 = jnp.float32
NEG_INF = float("-inf")


def _cparams(*sem):
    return pltpu.CompilerParams(dimension_semantics=sem, vmem_limit_bytes=VMEM_LIMIT_BYTES)


def _dot(a, b):
    return jnp.dot(a, b, preferred_element_type=F32)


def _dot_nt(a, b):
    return lax.dot_general(a, b, (((1,), (1,)), ((), ())), preferred_element_type=F32)


def _layer_norm(v, g, b):
    mu = jnp.mean(v, axis=-1, keepdims=True)
    d = v - mu
    var = jnp.mean(d * d, axis=-1, keepdims=True)
    return d * lax.rsqrt(var + LN_EPS) * g + b


def _rms_norm(v, g):
    return v * lax.rsqrt(jnp.mean(v * v, axis=-1, keepdims=True) + RMS_EPS) * g


def _sigmoid(v):
    return 1.0 / (1.0 + jnp.exp(-v))


def _full(shape):
    n = len(shape)
    return pl.BlockSpec(shape, lambda *_: (0,) * n)


def _conv_tail(conv, x, gcn, bcn, wpw2, bpw2, lg, lb):
    z = _layer_norm(conv, gcn, bcn)
    z = z * _sigmoid(z)
    mix = _dot(z.astype(BF16), wpw2) + bpw2
    return _layer_norm(ALPHA * x + mix, lg, lb)


def _conv_prompt_kernel(x_ref, wpw1_ref, bpw1_ref, w8_ref, bdw_ref, gcn_ref, bcn_ref, wpw2_ref, bpw2_ref,
                        lg_ref, lb_ref, y_ref, state_ref, ext_ref, conv_ref, *, ts, parts, row_chunk, lane_chunk):
    s = pl.program_id(1)
    base = ext_ref.at[0]

    @pl.when(s == 0)
    def _():
        base[0:HIST_ROWS, :] = jnp.zeros((HIST_ROWS, D_MODEL), F32)

    @pl.when(s > 0)
    def _():
        base[0:HIST_ROWS, :] = base[ts:ts + HIST_ROWS, :]

    groups = row_chunk // SUBLANES
    part = ts // parts

    def glu(r0):
        h = _dot(x_ref[r0:r0 + part, :].astype(BF16), wpw1_ref[...]) + bpw1_ref[...]
        base[HIST_ROWS + r0:HIST_ROWS + r0 + part, :] = h[:, :D_MODEL] * _sigmoid(h[:, D_MODEL:])

    def conv_rows(r0):
        span = part + HIST_ROWS - SUBLANES
        for j in range(1, SUBLANES):
            ext_ref[j, r0:r0 + span, :] = base[r0 + j:r0 + j + span, :]
        for c0 in range(r0, r0 + part, row_chunk):
            for l0 in range(0, D_MODEL, lane_chunk):
                acc = jnp.broadcast_to(bdw_ref[:, l0:l0 + lane_chunk][None], (groups, SUBLANES, lane_chunk))
                for k in range(CONV_WIDTH):
                    off = HIST_SKIP + k
                    j, a = off % SUBLANES, off // SUBLANES
                    start = c0 + SUBLANES * a
                    blk = ext_ref[j, start:start + row_chunk, l0:l0 + lane_chunk]
                    acc = acc + blk.reshape(groups, SUBLANES, lane_chunk) * w8_ref[k, :, l0:l0 + lane_chunk][None]
                conv_ref[c0:c0 + row_chunk, l0:l0 + lane_chunk] = acc.reshape(row_chunk, lane_chunk)
        y_ref[r0:r0 + part, :] = _conv_tail(conv_ref[r0:r0 + part, :], x_ref[r0:r0 + part, :], gcn_ref[...],
                                            bcn_ref[...], wpw2_ref[...], bpw2_ref[...], lg_ref[...], lb_ref[...])

    for p in range(parts):
        glu(p * part)
    for p in range(parts):
        conv_rows(p * part)

    @pl.when(s == pl.num_programs(1) - 1)
    def _():
        state_ref[...] = base[ts + HIST_SKIP:ts + HIST_ROWS, :]


def _conv_prompt(x, wpw1, bpw1, w8, bdw, gcn, bcn, wpw2, bpw2, lg, lb, *, ts=256):
    b, s, d = x.shape
    kern = functools.partial(_conv_prompt_kernel, ts=ts, parts=2, row_chunk=32, lane_chunk=256)
    return pl.pallas_call(
        kern,
        grid=(b, s // ts),
        in_specs=[
            pl.BlockSpec((None, ts, d), lambda i, j: (i, j, 0)),
            _full(wpw1.shape), _full(bpw1.shape), _full(w8.shape), _full(bdw.shape), _full(gcn.shape),
            _full(bcn.shape), _full(wpw2.shape), _full(bpw2.shape), _full(lg.shape), _full(lb.shape),
        ],
        out_specs=[
            pl.BlockSpec((None, ts, d), lambda i, j: (i, j, 0)),
            pl.BlockSpec((None, CONV_STATE, d), lambda i, j: (i, 0, 0)),
        ],
        out_shape=[
            jax.ShapeDtypeStruct((b, s, d), F32),
            jax.ShapeDtypeStruct((b, CONV_STATE, d), F32),
        ],
        scratch_shapes=[
            pltpu.VMEM((SUBLANES, ts + HIST_ROWS, d), F32),
            pltpu.VMEM((ts, d), F32),
        ],
        compiler_params=_cparams("arbitrary", "arbitrary"),
        name="conv_prompt",
    )(x, wpw1, bpw1, w8, bdw, gcn, bcn, wpw2, bpw2, lg, lb)


def _conv_sample_kernel(x_ref, past_ref, wpw1_ref, bpw1_ref, w8_ref, bdw_ref, gcn_ref, bcn_ref, wpw2_ref,
                        bpw2_ref, lg_ref, lb_ref, y_ref, state_ref, ext_ref, conv_ref, *, bb, t, lane_chunk):
    x = x_ref[...].reshape(bb * t, D_MODEL)
    h = _dot(x.astype(BF16), wpw1_ref[...]) + bpw1_ref[...]
    u = h[:, :D_MODEL] * _sigmoid(h[:, D_MODEL:])
    ext_ref[:, HIST_SKIP:HIST_ROWS, :] = past_ref[...]
    ext_ref[:, HIST_ROWS:HIST_ROWS + t, :] = u.reshape(bb, t, D_MODEL)
    for l0 in range(0, D_MODEL, lane_chunk):
        acc = jnp.broadcast_to(bdw_ref[:, l0:l0 + lane_chunk][None], (bb, t, lane_chunk))
        for k in range(CONV_WIDTH):
            off = HIST_SKIP + k
            acc = acc + ext_ref[:, off:off + t, l0:l0 + lane_chunk] * w8_ref[k, :, l0:l0 + lane_chunk][None]
        conv_ref[:, :, l0:l0 + lane_chunk] = acc
    y = _conv_tail(conv_ref[...].reshape(bb * t, D_MODEL), x, gcn_ref[...], bcn_ref[...], wpw2_ref[...],
                   bpw2_ref[...], lg_ref[...], lb_ref[...])
    y_ref[...] = y.reshape(bb, t, D_MODEL)
    state_ref[...] = ext_ref[:, HIST_SKIP + t:HIST_ROWS + t, :]


def _conv_sample(x, past, wpw1, bpw1, w8, bdw, gcn, bcn, wpw2, bpw2, lg, lb, *, bb=16):
    b, t, d = x.shape
    assert t == SUBLANES
    kern = functools.partial(_conv_sample_kernel, bb=bb, t=t, lane_chunk=256)
    return pl.pallas_call(
        kern,
        grid=(b // bb,),
        in_specs=[
            pl.BlockSpec((bb, t, d), lambda i: (i, 0, 0)),
            pl.BlockSpec((bb, CONV_STATE, d), lambda i: (i, 0, 0)),
            _full(wpw1.shape), _full(bpw1.shape), _full(w8.shape), _full(bdw.shape), _full(gcn.shape),
            _full(bcn.shape), _full(wpw2.shape), _full(bpw2.shape), _full(lg.shape), _full(lb.shape),
        ],
        out_specs=[
            pl.BlockSpec((bb, t, d), lambda i: (i, 0, 0)),
            pl.BlockSpec((bb, CONV_STATE, d), lambda i: (i, 0, 0)),
        ],
        out_shape=[
            jax.ShapeDtypeStruct((b, t, d), F32),
            jax.ShapeDtypeStruct((b, CONV_STATE, d), F32),
        ],
        scratch_shapes=[
            pltpu.VMEM((bb, HIST_ROWS + t, d), F32),
            pltpu.VMEM((bb, t, d), F32),
        ],
        compiler_params=_cparams("arbitrary"),
        name="conv_sample",
    )(x, past, wpw1, bpw1, w8, bdw, gcn, bcn, wpw2, bpw2, lg, lb)


def _router_kernel(x_ref, rwt_ref, rb_ref, gates_ref, cls_ref, pair_ref, cnt_ref):
    x = x_ref[...]
    w = rwt_ref[...]
    x_hi = x.astype(BF16)
    x_lo = (x - x_hi.astype(F32)).astype(BF16)
    w_hi = w.astype(BF16)
    w_lo = (w - w_hi.astype(F32)).astype(BF16)
    by_hi = _dot_nt(jnp.concatenate([w_hi, w_lo], axis=0), x_hi)
    logits = by_hi[:N_EXPERTS] + by_hi[N_EXPERTS:] + _dot_nt(w_hi, x_lo)
    aff = _sigmoid(logits)
    sel = aff + rb_ref[...]
    g = N_GROUPS
    s = [sel[m * g:(m + 1) * g] for m in range(EXPERTS_PER_GROUP)]
    a = [aff[m * g:(m + 1) * g] for m in range(EXPERTS_PER_GROUP)]
    hi01, lo01 = jnp.maximum(s[0], s[1]), jnp.minimum(s[0], s[1])
    hi23, lo23 = jnp.maximum(s[2], s[3]), jnp.minimum(s[2], s[3])
    top1 = jnp.maximum(hi01, hi23)
    top2 = jnp.maximum(jnp.minimum(hi01, hi23), jnp.maximum(lo01, lo23))
    score = top1 + top2
    best = score[0:1]
    best_idx = jnp.zeros(best.shape, jnp.int32)
    for gi in range(1, g):
        better = score[gi:gi + 1] > best
        best = jnp.where(better, score[gi:gi + 1], best)
        best_idx = jnp.where(better, gi, best_idx)
    in_group = lax.broadcasted_iota(jnp.int32, score.shape, 0) == best_idx
    picked, chosen = [], []
    for m in range(EXPERTS_PER_GROUP):
        rank = jnp.zeros(score.shape, jnp.int32)
        for j in range(EXPERTS_PER_GROUP):
            if j == m:
                continue
            ahead = (s[j] >= s[m]) if j < m else (s[j] > s[m])
            rank = rank + ahead.astype(jnp.int32)
        keep = in_group & (rank < 2)
        picked.append(jnp.where(keep, a[m], 0.0))
        chosen.append(jnp.where(keep, 1.0, 0.0))
    p = [jnp.sum(v, axis=0, keepdims=True) for v in picked]
    on = [jnp.sum(v, axis=0, keepdims=True) > 0.5 for v in chosen]
    inv = 1.0 / (p[0] + p[1] + p[2] + p[3])
    for m in range(EXPERTS_PER_GROUP):
        gates_ref[m * g:(m + 1) * g, :] = picked[m] * inv
    lo = jnp.where(on[0], 0, jnp.where(on[1], 1, 2))
    hi = jnp.where(on[3], 3, jnp.where(on[2], 2, 1))
    p_lo = jnp.where(on[0], p[0], jnp.where(on[1], p[1], p[2]))
    p_hi = jnp.where(on[3], p[3], jnp.where(on[2], p[2], p[1]))
    pair = jnp.where(lo == 0, hi - 1, jnp.where(lo == 1, hi + 1, len(MEMBER_PAIRS) - 1))
    cls = best_idx * len(MEMBER_PAIRS) + pair
    cls_ref[...] = cls

    @pl.when(pl.program_id(0) == 0)
    def _():
        cnt_ref[...] = jnp.zeros(cnt_ref.shape, F32)

    onehot = lax.broadcasted_iota(jnp.int32, (CLASS_ROWS, cls.shape[1]), 0) == cls
    cnt_ref[...] += jnp.sum(jnp.where(onehot, 1.0, 0.0), axis=1, keepdims=True)
    pair_ref[...] = jnp.zeros(pair_ref.shape, F32)
    pair_ref[0:1, :] = p_lo * inv
    pair_ref[1:2, :] = p_hi * inv


def _router(x2d, rwt, rb, *, tm=1024):
    t, d = x2d.shape
    return pl.pallas_call(
        _router_kernel,
        grid=(t // tm,),
        in_specs=[pl.BlockSpec((tm, d), lambda i: (i, 0)), _full(rwt.shape), _full(rb.shape)],
        out_specs=[
            pl.BlockSpec((N_EXPERTS, tm), lambda i: (0, i)),
            pl.BlockSpec((1, tm), lambda i: (0, i)),
            pl.BlockSpec((SUBLANES, tm), lambda i: (0, i)),
            pl.BlockSpec((CLASS_ROWS, LANES), lambda i: (0, 0)),
        ],
        out_shape=[
            jax.ShapeDtypeStruct((N_EXPERTS, t), F32),
            jax.ShapeDtypeStruct((1, t), jnp.int32),
            jax.ShapeDtypeStruct((SUBLANES, t), F32),
            jax.ShapeDtypeStruct((CLASS_ROWS, LANES), F32),
        ],
        compiler_params=_cparams("arbitrary"),
        name="router",
    )(x2d, rwt, rb)


def _positions_kernel(cls_ref, cnt_ref, pos_ref, tcls_ref, nt_ref, start_ref, run_ref, *, tb, tm):
    i = pl.program_id(0)
    shift = tm.bit_length() - 1
    onehot = lax.broadcasted_iota(jnp.int32, (CLASS_ROWS, tb), 0) == cls_ref[...]
    ohf = jnp.where(onehot, 1.0, 0.0)

    @pl.when(i == 0)
    def _():
        cnt = cnt_ref[...].astype(jnp.int32)
        padded = (((cnt + (tm - 1)) >> shift) << shift).astype(F32)
        r = lax.broadcasted_iota(jnp.int32, (CLASS_ROWS, CLASS_ROWS), 0)
        c = lax.broadcasted_iota(jnp.int32, (CLASS_ROWS, CLASS_ROWS), 1)
        start = jnp.dot(jnp.where(c < r, 1.0, 0.0), padded, preferred_element_type=F32,
                        precision=lax.Precision.HIGHEST)
        start_ref[...] = start
        run_ref[...] = jnp.zeros(run_ref.shape, F32)
        tile_start = (lax.broadcasted_iota(jnp.int32, (CLASS_ROWS, LANES), 1) << shift).astype(F32)
        real = lax.broadcasted_iota(jnp.int32, (CLASS_ROWS, LANES), 0) < N_CLASSES
        below = jnp.where(real, jnp.where(start <= tile_start, 1.0, 0.0), 0.0)
        tcls_ref[...] = jnp.sum(below, axis=0, keepdims=True).astype(jnp.int32) - 1
        nt_ref[...] = jnp.sum(padded, axis=0, keepdims=True).astype(jnp.int32) >> shift

    rr = lax.broadcasted_iota(jnp.int32, (tb, tb), 0)
    cc = lax.broadcasted_iota(jnp.int32, (tb, tb), 1)
    upper = jnp.where(rr <= cc, 1.0, 0.0).astype(BF16)
    cum = _dot(ohf.astype(BF16), upper)
    base = run_ref[:, 0:1] + start_ref[:, 0:1] - 1.0
    pos_ref[...] = jnp.sum(ohf * (cum + base), axis=0, keepdims=True).astype(jnp.int32)
    run_ref[...] += cum[:, tb - 1:tb]


def _positions(cls, cnt, *, tb=512, tm=MOE_TILE):
    t = cls.shape[1]
    assert tm & (tm - 1) == 0 and (t + N_CLASSES * tm) // tm <= LANES
    kern = functools.partial(_positions_kernel, tb=tb, tm=tm)
    return pl.pallas_call(
        kern,
        grid=(t // tb,),
        in_specs=[pl.BlockSpec((1, tb), lambda i: (0, i)), _full(cnt.shape)],
        out_specs=[
            pl.BlockSpec((1, tb), lambda i: (0, i)),
            pl.BlockSpec((1, LANES), lambda i: (0, 0)),
            pl.BlockSpec((1, LANES), lambda i: (0, 0)),
        ],
        out_shape=[
            jax.ShapeDtypeStruct((1, t), jnp.int32),
            jax.ShapeDtypeStruct((1, LANES), jnp.int32),
            jax.ShapeDtypeStruct((1, LANES), jnp.int32),
        ],
        scratch_shapes=[pltpu.VMEM((CLASS_ROWS, LANES), F32)] * 2,
        compiler_params=_cparams("arbitrary"),
        name="positions",
    )(cls, cnt)


def _staged_row(ref, c, u):
    return ref.at[c, pl.ds(u, 1), :]


def _hbm_row(ref, r):
    return ref.at[r >> (SUBLANES.bit_length() - 1), pl.ds(r & (SUBLANES - 1), 1), :]


def _row_scatter_kernel(pos_ref, x_ref, pair_ref, init_hbm, dst_hbm, aug_ref, gcol_ref, sem, *, tb):
    del init_hbm
    groups = tb // SUBLANES
    aug_ref[:, :, :D_MODEL] = x_ref[...]
    gcol_ref[...] = jnp.zeros(gcol_ref.shape, F32)
    gcol_ref[0:SUBLANES, :] = pair_ref[...]
    aug_ref[:, :, D_MODEL:] = gcol_ref[...].T.reshape(groups, SUBLANES, GATE_PAD)

    def issue(c, carry):
        for u in range(SUBLANES):
            dst = _hbm_row(dst_hbm, pos_ref[0, c * SUBLANES + u])
            pltpu.make_async_copy(_staged_row(aug_ref, c, u), dst, sem).start(priority=u % 2)
        return carry

    lax.fori_loop(0, groups, issue, 0)
    pltpu.make_async_copy(aug_ref, dst_hbm.at[pl.ds(0, groups)], sem).wait()


def _row_scatter(pos3, x2d, pair, init):
    nb, _, tb = pos3.shape
    t, d = x2d.shape
    rows, da = init.shape
    kern = functools.partial(_row_scatter_kernel, tb=tb)
    out = pl.pallas_call(
        kern,
        grid=(nb,),
        in_specs=[
            pl.BlockSpec((None, 1, tb), lambda i: (i, 0, 0), memory_space=pltpu.SMEM),
            pl.BlockSpec((tb // SUBLANES, SUBLANES, d), lambda i: (i, 0, 0)),
            pl.BlockSpec((SUBLANES, tb), lambda i: (0, i)),
            pl.BlockSpec(memory_space=pl.ANY),
        ],
        out_specs=pl.BlockSpec(memory_space=pl.ANY),
        out_shape=jax.ShapeDtypeStruct((rows // SUBLANES, SUBLANES, da), init.dtype),
        scratch_shapes=[
            pltpu.VMEM((tb // SUBLANES, SUBLANES, da), F32),
            pltpu.VMEM((GATE_PAD, tb), F32),
            pltpu.SemaphoreType.DMA,
        ],
        input_output_aliases={3: 0},
        compiler_params=_cparams("arbitrary"),
        name="row_scatter",
    )(pos3, x2d.reshape(t // SUBLANES, SUBLANES, d), pair, init.reshape(rows // SUBLANES, SUBLANES, da))
    return out.reshape(rows, da)


def _row_gather_kernel(pos_ref, src_hbm, y_ref, sem, *, tb):
    groups = tb // SUBLANES

    def issue(c, carry):
        for u in range(SUBLANES):
            src = _hbm_row(src_hbm, pos_ref[0, c * SUBLANES + u])
            pltpu.make_async_copy(src, _staged_row(y_ref, c, u), sem).start(priority=u % 2)
        return carry

    lax.fori_loop(0, groups, issue, 0)
    pltpu.make_async_copy(src_hbm.at[pl.ds(0, groups)], y_ref, sem).wait()


def _row_gather(pos3, src):
    nb, _, tb = pos3.shape
    rows, d = src.shape
    kern = functools.partial(_row_gather_kernel, tb=tb)
    out = pl.pallas_call(
        kern,
        grid=(nb,),
        in_specs=[
            pl.BlockSpec((None, 1, tb), lambda i: (i, 0, 0), memory_space=pltpu.SMEM),
            pl.BlockSpec(memory_space=pl.ANY),
        ],
        out_specs=pl.BlockSpec((tb // SUBLANES, SUBLANES, d), lambda i: (i, 0, 0)),
        out_shape=jax.ShapeDtypeStruct((nb * tb // SUBLANES, SUBLANES, d), src.dtype),
        scratch_shapes=[pltpu.SemaphoreType.DMA],
        compiler_params=_cparams("arbitrary"),
        name="row_gather",
    )(pos3, src.reshape(rows // SUBLANES, SUBLANES, d))
    return out.reshape(nb * tb, d)


def _moe_sorted_kernel(ea_ref, eb_ref, nt_ref, xs_ref, w1a_ref, w3a_ref, w2a_ref, w1b_ref, w3b_ref, w2b_ref,
                       lg_ref, lb_ref, y_ref, *wb_refs):
    i = pl.program_id(0)
    used = i < nt_ref[0]
    prev = jnp.maximum(i - 1, 0)
    new_pair = (i == 0) | (ea_ref[i] != ea_ref[prev]) | (eb_ref[i] != eb_ref[prev])

    @pl.when(jnp.logical_not(used))
    def _():
        y_ref[...] = jnp.zeros(y_ref.shape, F32)

    @pl.when(used & new_pair)
    def _():
        for src, dst in zip((w1a_ref, w3a_ref, w2a_ref, w1b_ref, w3b_ref, w2b_ref), wb_refs):
            dst[...] = src[...].astype(BF16)

    @pl.when(used)
    def _():
        x = xs_ref[:, :D_MODEL]
        xb = x.astype(BF16)

        def ffn(w1_ref, w3_ref, w2_ref):
            h1 = _dot(xb, w1_ref[...])
            h3 = _dot(xb, w3_ref[...])
            return _dot((h1 * _sigmoid(h1) * h3).astype(BF16), w2_ref[...])

        out = xs_ref[:, D_MODEL:D_MODEL + 1] * ffn(*wb_refs[:3])
        out = out + xs_ref[:, D_MODEL + 1:D_MODEL + 2] * ffn(*wb_refs[3:])
        y_ref[...] = _layer_norm(ALPHA * x + out, lg_ref[...], lb_ref[...])


def _moe_sorted(ea, eb, nt, xs, layer, w1, w3, w2, lg, lb, *, tm=MOE_TILE):
    p, da = xs.shape
    _, _, d, f = w1.shape

    def row_block(i, ea, eb, nt):
        return (i, 0)

    def wa(i, ea, eb, nt):
        return (layer, ea[i], 0, 0)

    def wb(i, ea, eb, nt):
        return (layer, eb[i], 0, 0)

    grid_spec = pltpu.PrefetchScalarGridSpec(
        num_scalar_prefetch=3,
        grid=(p // tm,),
        in_specs=[
            pl.BlockSpec((tm, da), row_block),
            pl.BlockSpec((None, None, d, f), wa), pl.BlockSpec((None, None, d, f), wa),
            pl.BlockSpec((None, None, f, d), wa),
            pl.BlockSpec((None, None, d, f), wb), pl.BlockSpec((None, None, d, f), wb),
            pl.BlockSpec((None, None, f, d), wb),
            pl.BlockSpec(lg.shape, lambda i, ea, eb, nt: (0, 0)),
            pl.BlockSpec(lb.shape, lambda i, ea, eb, nt: (0, 0)),
        ],
        out_specs=pl.BlockSpec((tm, d), row_block),
        scratch_shapes=[pltpu.VMEM((d, f), BF16), pltpu.VMEM((d, f), BF16), pltpu.VMEM((f, d), BF16)] * 2,
    )
    return pl.pallas_call(
        _moe_sorted_kernel,
        grid_spec=grid_spec,
        out_shape=jax.ShapeDtypeStruct((p, d), F32),
        compiler_params=_cparams("arbitrary"),
        name="moe_sorted",
    )(ea, eb, nt, xs, w1, w3, w2, w1, w3, w2, lg, lb)


def _moe_kernel(x_ref, gates_ref, w1_ref, w3_ref, w2_ref, lg_ref, lb_ref, y_ref, xb_ref, acc_ref):
    e = pl.program_id(1)

    @pl.when(e == 0)
    def _():
        xb_ref[...] = x_ref[...].astype(BF16)
        acc_ref[...] = jnp.zeros(acc_ref.shape, F32)

    xb = xb_ref[...]
    h1 = _dot(xb, w1_ref[...].astype(BF16))
    h3 = _dot(xb, w3_ref[...].astype(BF16))
    h = (h1 * _sigmoid(h1) * h3).astype(BF16)
    lane = lax.broadcasted_iota(jnp.int32, gates_ref.shape, 1)
    gate = jnp.sum(jnp.where(lane == e, gates_ref[...], 0.0), axis=1, keepdims=True)
    acc_ref[...] += gate * _dot(h, w2_ref[...].astype(BF16))

    @pl.when(e == pl.num_programs(1) - 1)
    def _():
        y_ref[...] = _layer_norm(ALPHA * x_ref[...] + acc_ref[...], lg_ref[...], lb_ref[...])


def _moe(x2d, gates, layer, w1, w3, w2, lg, lb, *, tm=1024):
    t, d = x2d.shape
    _, e, _, f = w1.shape
    return pl.pallas_call(
        _moe_kernel,
        grid=(t // tm, e),
        in_specs=[
            pl.BlockSpec((tm, d), lambda i, j: (i, 0)),
            pl.BlockSpec((tm, e), lambda i, j: (i, 0)),
            pl.BlockSpec((None, None, d, f), lambda i, j: (layer, j, 0, 0)),
            pl.BlockSpec((None, None, d, f), lambda i, j: (layer, j, 0, 0)),
            pl.BlockSpec((None, None, f, d), lambda i, j: (layer, j, 0, 0)),
            _full(lg.shape), _full(lb.shape),
        ],
        out_specs=pl.BlockSpec((tm, d), lambda i, j: (i, 0)),
        out_shape=jax.ShapeDtypeStruct((t, d), F32),
        scratch_shapes=[pltpu.VMEM((tm, d), BF16), pltpu.VMEM((tm, d), F32)],
        compiler_params=_cparams("arbitrary", "arbitrary"),
        name="moe",
    )(x2d, gates, w1, w3, w2, lg, lb)


def _shared_kv_kernel(x_ref, wdkv_ref, gkv_ref, wkr_ref, wkrs_ref, cos_ref, sin_ref,
                      ckv_ref, kpe_ref, ckvb_ref, kpeb_ref):
    xb = x_ref[...].astype(BF16)
    ckv = _rms_norm(_dot(xb, wdkv_ref[...]), gkv_ref[...])
    kpe = _dot(xb, wkr_ref[...]) * cos_ref[...] + _dot(xb, wkrs_ref[...]) * sin_ref[...]
    ckv_ref[...] = ckv
    kpe_ref[...] = kpe[:, :QK_ROPE]
    ckvb_ref[...] = ckv.astype(BF16)
    kpeb_ref[...] = kpe.astype(BF16)


def _shared_kv(x2d, wdkv, gkv, wkr, wkrs, cos, sin, *, tm, table_blocks):
    t, d = x2d.shape
    tab = pl.BlockSpec((tm, ROPE_PAD), lambda i: (i % table_blocks, 0))
    return pl.pallas_call(
        _shared_kv_kernel,
        grid=(t // tm,),
        in_specs=[pl.BlockSpec((tm, d), lambda i: (i, 0)), _full(wdkv.shape), _full(gkv.shape),
                  _full(wkr.shape), _full(wkrs.shape), tab, tab],
        out_specs=[
            pl.BlockSpec((tm, KV_LORA), lambda i: (i, 0)),
            pl.BlockSpec((tm, QK_ROPE), lambda i: (i, 0)),
            pl.BlockSpec((tm, KV_LORA), lambda i: (i, 0)),
            pl.BlockSpec((tm, ROPE_PAD), lambda i: (i, 0)),
        ],
        out_shape=[
            jax.ShapeDtypeStruct((t, KV_LORA), F32),
            jax.ShapeDtypeStruct((t, QK_ROPE), F32),
            jax.ShapeDtypeStruct((t, KV_LORA), BF16),
            jax.ShapeDtypeStruct((t, ROPE_PAD), BF16),
        ],
        compiler_params=_cparams("arbitrary"),
        name="shared_kv",
    )(x2d, wdkv, gkv, wkr, wkrs, cos, sin)


def _project_queries(x, cos, sin, wdq, gq, wuqn, wuqp, wuqps, wukt_ref, store):
    cq = _rms_norm(_dot(x.astype(BF16), wdq), gq).astype(BF16)
    qn = _dot(cq, wuqn)
    qp = _dot(cq, wuqp)
    qps = _dot(cq, wuqps)
    cos, sin = cos * SCORE_SCALE, sin * SCORE_SCALE
    for h in range(N_HEADS):
        lat = _dot(qn[:, h * QK_NOPE:(h + 1) * QK_NOPE].astype(BF16), wukt_ref[h]) * SCORE_SCALE
        sl = slice(h * ROPE_PAD, (h + 1) * ROPE_PAD)
        pe = qp[:, sl] * cos + qps[:, sl] * sin
        store(h, lat.astype(BF16), pe.astype(BF16))


def _project_output(o_heads, x, wuv_ref, wo, lg, lb):
    o = jnp.concatenate([_dot(o_heads[h].astype(BF16), wuv_ref[h]) for h in range(N_HEADS)], axis=-1)
    mix = _dot(o.astype(BF16), wo)
    return _layer_norm(ALPHA * x + mix, lg, lb)


def _attn_prompt_kernel(x_ref, ckv_ref, kpe_ref, cos_ref, sin_ref, wdq_ref, gq_ref, wuqn_ref, wuqp_ref,
                        wuqps_ref, wukt_ref, wuv_ref, wo_ref, lg_ref, lb_ref, y_ref,
                        ql_ref, qp_ref, m_ref, l_ref, acc_ref, *, tq, tk, group_heads):
    i = pl.program_id(1)
    x = x_ref[...]

    def store(h, lat, pe):
        ql_ref[h * tq:(h + 1) * tq, :] = lat
        qp_ref[h * tq:(h + 1) * tq, :] = pe

    _project_queries(x, cos_ref[...], sin_ref[...], wdq_ref[...], gq_ref[...], wuqn_ref[...], wuqp_ref[...],
                     wuqps_ref[...], wukt_ref, store)

    rows = N_HEADS * tq
    m_ref[...] = jnp.full((rows, 1), NEG_INF, F32)
    l_ref[...] = jnp.zeros((rows, 1), F32)
    acc_ref[...] = jnp.zeros((rows, KV_LORA), F32)

    group = group_heads * tq

    def block(kb, masked):
        k0 = pl.multiple_of(kb * tk, tk)
        kc = ckv_ref[pl.ds(k0, tk), :]
        kp = kpe_ref[pl.ds(k0, tk), :]
        for r0 in range(0, rows, group):
            rs = slice(r0, r0 + group)
            s = _dot_nt(ql_ref[rs, :], kc) + _dot_nt(qp_ref[rs, :], kp)
            if masked:
                r = lax.broadcasted_iota(jnp.int32, (group, tk), 0) & (tq - 1)
                c = lax.broadcasted_iota(jnp.int32, (group, tk), 1)
                s = jnp.where(k0 + c <= i * tq + r, s, NEG_INF)
            m_old = m_ref[rs, :]
            m_new = jnp.maximum(m_old, jnp.max(s, axis=-1, keepdims=True))
            p = jnp.exp(s - m_new)
            scale = jnp.exp(m_old - m_new)
            l_ref[rs, :] = scale * l_ref[rs, :] + jnp.sum(p, axis=-1, keepdims=True)
            acc_ref[rs, :] = scale * acc_ref[rs, :] + _dot(p.astype(BF16), kc)
            m_ref[rs, :] = m_new

    n_full = (i * tq) // tk

    def body(kb, carry):
        block(kb, False)
        return carry

    lax.fori_loop(0, n_full, body, 0)
    block(n_full, True)

    inv = 1.0 / l_ref[...]
    o_heads = [acc_ref[h * tq:(h + 1) * tq, :] * inv[h * tq:(h + 1) * tq] for h in range(N_HEADS)]
    y_ref[...] = _project_output(o_heads, x, wuv_ref, wo_ref[...], lg_ref[...], lb_ref[...])


def _attn_prompt(x, ckvb, kpeb, cos, sin, wdq, gq, wuqn, wuqp, wuqps, wukt, wuv, wo, lg, lb, *, tq=256, tk=512,
                 group_heads=2):
    b, s, d = x.shape
    assert tk % tq == 0 and s % tk == 0 and tq & (tq - 1) == 0 and N_HEADS % group_heads == 0
    kern = functools.partial(_attn_prompt_kernel, tq=tq, tk=tk, group_heads=group_heads)
    rows = N_HEADS * tq
    return pl.pallas_call(
        kern,
        grid=(b, s // tq),
        in_specs=[
            pl.BlockSpec((None, tq, d), lambda i, j: (i, j, 0)),
            pl.BlockSpec((None, s, KV_LORA), lambda i, j: (i, 0, 0)),
            pl.BlockSpec((None, s, ROPE_PAD), lambda i, j: (i, 0, 0)),
            pl.BlockSpec((tq, ROPE_PAD), lambda i, j: (j, 0)),
            pl.BlockSpec((tq, ROPE_PAD), lambda i, j: (j, 0)),
            _full(wdq.shape), _full(gq.shape), _full(wuqn.shape), _full(wuqp.shape), _full(wuqps.shape),
            _full(wukt.shape), _full(wuv.shape), _full(wo.shape), _full(lg.shape), _full(lb.shape),
        ],
        out_specs=pl.BlockSpec((None, tq, d), lambda i, j: (i, j, 0)),
        out_shape=jax.ShapeDtypeStruct((b, s, d), F32),
        scratch_shapes=[
            pltpu.VMEM((rows, KV_LORA), BF16),
            pltpu.VMEM((rows, ROPE_PAD), BF16),
            pltpu.VMEM((rows, 1), F32),
            pltpu.VMEM((rows, 1), F32),
            pltpu.VMEM((rows, KV_LORA), F32),
        ],
        compiler_params=_cparams("arbitrary", "arbitrary"),
        name="attn_prompt",
    )(x, ckvb, kpeb, cos, sin, wdq, gq, wuqn, wuqp, wuqps, wukt, wuv, wo, lg, lb)


def _q_sample_kernel(x_ref, cos_ref, sin_ref, wdq_ref, gq_ref, wuqn_ref, wuqp_ref, wuqps_ref, wukt_ref,
                     ql_ref, qp_ref, *, bb, t):
    x = x_ref[...].reshape(bb * t, D_MODEL)

    def store(h, lat, pe):
        ql_ref[:, h] = lat.reshape(bb, t, KV_LORA)
        qp_ref[:, h] = pe.reshape(bb, t, ROPE_PAD)

    _project_queries(x, cos_ref[...], sin_ref[...], wdq_ref[...], gq_ref[...], wuqn_ref[...], wuqp_ref[...],
                     wuqps_ref[...], wukt_ref, store)


def _q_sample(x, cos, sin, wdq, gq, wuqn, wuqp, wuqps, wukt, *, bb=32):
    b, t, d = x.shape
    kern = functools.partial(_q_sample_kernel, bb=bb, t=t)
    return pl.pallas_call(
        kern,
        grid=(b // bb,),
        in_specs=[
            pl.BlockSpec((bb, t, d), lambda i: (i, 0, 0)),
            pl.BlockSpec((bb * t, ROPE_PAD), lambda i: (i, 0)),
            pl.BlockSpec((bb * t, ROPE_PAD), lambda i: (i, 0)),
            _full(wdq.shape), _full(gq.shape), _full(wuqn.shape), _full(wuqp.shape), _full(wuqps.shape),
            _full(wukt.shape),
        ],
        out_specs=[
            pl.BlockSpec((bb, N_HEADS, t, KV_LORA), lambda i: (i, 0, 0, 0)),
            pl.BlockSpec((bb, N_HEADS, t, ROPE_PAD), lambda i: (i, 0, 0, 0)),
        ],
        out_shape=[
            jax.ShapeDtypeStruct((b, N_HEADS, t, KV_LORA), BF16),
            jax.ShapeDtypeStruct((b, N_HEADS, t, ROPE_PAD), BF16),
        ],
        compiler_params=_cparams("arbitrary"),
        name="q_sample",
    )(x, cos, sin, wdq, gq, wuqn, wuqp, wuqps, wukt)


def _attn_sample_kernel(pt_ref, ql_ref, qp_ref, cnew_ref, pnew_ref, ckv_hbm, kpe_hbm, o_ref,
                        cbuf_ref, pbuf_ref, kc_ref, kp_ref, sem, *, pages, page, t):
    b = pl.program_id(0)
    nb = pl.num_programs(0)
    slot = b % 2
    rows = ql_ref.shape[0]

    def fetch(seq, into):
        for p in range(pages):
            src = pt_ref[seq * pages + p]
            pltpu.make_async_copy(ckv_hbm.at[src], cbuf_ref.at[into, p], sem.at[0, into]).start()
            pltpu.make_async_copy(kpe_hbm.at[src], pbuf_ref.at[into, p], sem.at[1, into]).start()

    @pl.when(b == 0)
    def _():
        fetch(0, 0)

    @pl.when(b + 1 < nb)
    def _():
        fetch(b + 1, 1 - slot)

    pltpu.make_async_copy(ckv_hbm.at[pl.ds(0, pages)], cbuf_ref.at[slot], sem.at[0, slot]).wait()
    pltpu.make_async_copy(kpe_hbm.at[pl.ds(0, pages)], pbuf_ref.at[slot], sem.at[1, slot]).wait()

    for p in range(pages):
        kc_ref[p * page:(p + 1) * page, :] = cbuf_ref[slot, p].astype(BF16)
        kp_ref[:, p * page:(p + 1) * page] = pbuf_ref[slot, p].astype(BF16)

    ql = ql_ref[...]
    qp = qp_ref[...]
    kc = kc_ref[...]
    cn = cnew_ref[...]
    s_past = _dot_nt(ql, kc) + _dot(qp[:, :QK_ROPE], kp_ref[...])
    s_new = _dot_nt(ql, cn) + _dot_nt(qp, pnew_ref[...])
    n = cn.shape[0]
    qpos = lax.broadcasted_iota(jnp.int32, (rows, n), 0) & (t - 1)
    kpos = lax.broadcasted_iota(jnp.int32, (rows, n), 1)
    s_new = jnp.where(kpos <= qpos, s_new, NEG_INF)
    m = jnp.maximum(jnp.max(s_past, axis=-1, keepdims=True), jnp.max(s_new, axis=-1, keepdims=True))
    p_past = jnp.exp(s_past - m)
    p_new = jnp.exp(s_new - m)
    l = jnp.sum(p_past, axis=-1, keepdims=True) + jnp.sum(p_new, axis=-1, keepdims=True)
    o_ref[...] = (_dot(p_past.astype(BF16), kc) + _dot(p_new.astype(BF16), cn)) * (1.0 / l)


def _attn_sample(page_table, ql, qp, cnew, pnew, cache_ckv, cache_kpe_t):
    b, rows, _ = ql.shape
    pages = page_table.shape[1]
    page = cache_ckv.shape[1]
    t = rows // N_HEADS
    n_new = cnew.shape[1]
    kern = functools.partial(_attn_sample_kernel, pages=pages, page=page, t=t)
    grid_spec = pltpu.PrefetchScalarGridSpec(
        num_scalar_prefetch=1,
        grid=(b,),
        in_specs=[
            pl.BlockSpec((None, rows, KV_LORA), lambda i, pt: (i, 0, 0)),
            pl.BlockSpec((None, rows, ROPE_PAD), lambda i, pt: (i, 0, 0)),
            pl.BlockSpec((None, n_new, KV_LORA), lambda i, pt: (i, 0, 0)),
            pl.BlockSpec((None, n_new, ROPE_PAD), lambda i, pt: (i, 0, 0)),
            pl.BlockSpec(memory_space=pl.ANY),
            pl.BlockSpec(memory_space=pl.ANY),
        ],
        out_specs=pl.BlockSpec((None, rows, KV_LORA), lambda i, pt: (i, 0, 0)),
        scratch_shapes=[
            pltpu.VMEM((2, pages, page, KV_LORA), F32),
            pltpu.VMEM((2, pages, QK_ROPE, page), F32),
            pltpu.VMEM((pages * page, KV_LORA), BF16),
            pltpu.VMEM((QK_ROPE, pages * page), BF16),
            pltpu.SemaphoreType.DMA((2, 2)),
        ],
    )
    return pl.pallas_call(
        kern,
        grid_spec=grid_spec,
        out_shape=jax.ShapeDtypeStruct((b, rows, KV_LORA), F32),
        compiler_params=_cparams("arbitrary"),
        name="attn_sample",
    )(page_table.reshape(-1), ql, qp, cnew, pnew, cache_ckv, cache_kpe_t)


def _o_sample_kernel(o_ref, x_ref, wuv_ref, wo_ref, lg_ref, lb_ref, y_ref, *, bb, t):
    x = x_ref[...].reshape(bb * t, D_MODEL)
    o_heads = [o_ref[:, h].reshape(bb * t, KV_LORA) for h in range(N_HEADS)]
    y_ref[...] = _project_output(o_heads, x, wuv_ref, wo_ref[...], lg_ref[...], lb_ref[...]).reshape(bb, t, D_MODEL)


def _o_sample(o_lat, x, wuv, wo, lg, lb, *, bb=32):
    b, t, d = x.shape
    kern = functools.partial(_o_sample_kernel, bb=bb, t=t)
    return pl.pallas_call(
        kern,
        grid=(b // bb,),
        in_specs=[
            pl.BlockSpec((bb, N_HEADS, t, KV_LORA), lambda i: (i, 0, 0, 0)),
            pl.BlockSpec((bb, t, d), lambda i: (i, 0, 0)),
            _full(wuv.shape), _full(wo.shape), _full(lg.shape), _full(lb.shape),
        ],
        out_specs=pl.BlockSpec((bb, t, d), lambda i: (i, 0, 0)),
        out_shape=jax.ShapeDtypeStruct((b, t, d), F32),
        compiler_params=_cparams("arbitrary"),
        name="o_sample",
    )(o_lat, x, wuv, wo, lg, lb)


def _rope_tables(pos):
    half = QK_ROPE // 2
    inv_freq = ROPE_BASE ** (-jnp.arange(half, dtype=F32) / half)
    ang = pos.astype(F32)[:, None] * inv_freq[None, :]
    cos, sin = jnp.cos(ang), jnp.sin(ang)
    pad = jnp.zeros((pos.shape[0], ROPE_PAD - QK_ROPE), F32)
    return (jnp.concatenate([cos, cos, pad], axis=-1), jnp.concatenate([-sin, sin, pad], axis=-1))


def _swap_halves(w):
    half = QK_ROPE // 2
    return jnp.concatenate([w[..., half:], w[..., :half]], axis=-1)


def _pad_rope(w):
    return jnp.pad(w, [(0, 0)] * (w.ndim - 1) + [(0, ROPE_PAD - QK_ROPE)])


def _row(v):
    return v.reshape(1, -1)


def kernel(x_prompt, x_sample, state_conv, cache_ckv, cache_kpe, page_table, a_w_pw1, a_b_pw1, a_w_dw, a_b_dw, a_g_cn, a_b_cn, a_w_pw2, a_b_pw2, ln_mix_g, ln_mix_b, ln_ffn_g, ln_ffn_b, b_w_dq, b_g_q, b_w_uq, b_w_o, s_w_dkv, s_g_kv, s_w_kr, s_w_uk, s_w_uv, r_w, r_b, e_w1, e_w3, e_w2):
    bp, sp, d = x_prompt.shape
    bs, ts, _ = x_sample.shape
    past_len = page_table.shape[1] * cache_ckv.shape[1]

    a_w_pw1b, a_w_pw2b = a_w_pw1.astype(BF16), a_w_pw2.astype(BF16)
    w8 = jnp.broadcast_to(a_w_dw[:, :, None, :], (N_A_LAYERS, CONV_WIDTH, SUBLANES, d))
    perm = jnp.arange(N_EXPERTS).reshape(N_GROUPS, EXPERTS_PER_GROUP).T.reshape(-1)
    rwt = r_w.T[perm]
    rbp = r_b[perm].reshape(N_EXPERTS, 1)
    wdkv = s_w_dkv.astype(BF16)
    wkr = _pad_rope(s_w_kr).astype(BF16)
    wkrs = _pad_rope(_swap_halves(s_w_kr)).astype(BF16)
    wukt = jnp.transpose(s_w_uk, (1, 2, 0)).astype(BF16)
    wuv = jnp.transpose(s_w_uv, (1, 0, 2)).astype(BF16)
    wdq = b_w_dq.astype(BF16)
    uq_pe = b_w_uq[..., QK_NOPE:]
    wuqn = b_w_uq[..., :QK_NOPE].reshape(-1, Q_LORA, N_HEADS * QK_NOPE).astype(BF16)
    wuqp = _pad_rope(uq_pe).reshape(-1, Q_LORA, N_HEADS * ROPE_PAD).astype(BF16)
    wuqps = _pad_rope(_swap_halves(uq_pe)).reshape(-1, Q_LORA, N_HEADS * ROPE_PAD).astype(BF16)
    wo = b_w_o.astype(BF16)

    cos_p, sin_p = _rope_tables(jnp.arange(sp))
    cos_s1, sin_s1 = _rope_tables(past_len + jnp.arange(ts))
    cos_s, sin_s = jnp.tile(cos_s1, (bs, 1)), jnp.tile(sin_s1, (bs, 1))

    cls_lo = jnp.array([g * EXPERTS_PER_GROUP + lo for g in range(N_GROUPS) for lo, _ in MEMBER_PAIRS], jnp.int32)
    cls_hi = jnp.array([g * EXPERTS_PER_GROUP + hi for g in range(N_GROUPS) for _, hi in MEMBER_PAIRS], jnp.int32)

    sorted_rows = []

    def moe_block(x, l, routed):
        x2d = x.reshape(-1, d)
        t = x2d.shape[0]
        ffn = (l, e_w1, e_w3, e_w2, _row(ln_ffn_g[l]), _row(ln_ffn_b[l]))
        gates_t, cls, pair, cnt = _router(x2d, rwt, rbp)
        if not routed:
            gates = gates_t.reshape(EXPERTS_PER_GROUP, N_GROUPS, -1).transpose(2, 1, 0).reshape(-1, N_EXPERTS)
            return _moe(x2d, gates, *ffn).reshape(x.shape)
        pos, tile_cls, n_tiles = _positions(cls, cnt)
        rows = t + N_CLASSES * MOE_TILE
        nt = n_tiles[0, :1]
        tile = jnp.arange(rows // MOE_TILE)
        tc = jnp.clip(tile_cls[0, jnp.minimum(tile, nt[0] - 1)], 0, N_CLASSES - 1)
        pos3 = pos.reshape(-1, 1, 2048)
        init = sorted_rows[0] if sorted_rows else jnp.zeros((rows, d + GATE_PAD), F32)
        xs = _row_scatter(pos3, x2d, pair, init)
        sorted_rows[:] = [xs]
        ys = _moe_sorted(cls_lo[tc], cls_hi[tc], nt, xs, *ffn)
        return _row_gather(pos3, ys).reshape(x.shape)

    def conv_args(l):
        return (a_w_pw1b[l], _row(a_b_pw1[l]), w8[l], _row(a_b_dw[l]), _row(a_g_cn[l]), _row(a_b_cn[l]),
                a_w_pw2b[l], _row(a_b_pw2[l]), _row(ln_mix_g[l]), _row(ln_mix_b[l]))

    def q_args(j):
        return (wdq[j], _row(b_g_q[j]), wuqn[j], wuqp[j], wuqps[j], wukt)

    x = x_prompt
    conv_prompt = []
    for l in range(N_A_LAYERS):
        x, st = _conv_prompt(x, *conv_args(l))
        conv_prompt.append(st)
        x = moe_block(x, l, True)
    ckv_p, kpe_p, ckvb_p, kpeb_p = _shared_kv(x.reshape(-1, d), wdkv, _row(s_g_kv), wkr, wkrs, cos_p, sin_p,
                                              tm=512, table_blocks=sp // 512)
    ckvb_p3, kpeb_p3 = ckvb_p.reshape(bp, sp, KV_LORA), kpeb_p.reshape(bp, sp, ROPE_PAD)
    for l in range(N_A_LAYERS, DEPTH):
        j = l - N_A_LAYERS
        x = _attn_prompt(x, ckvb_p3, kpeb_p3, cos_p, sin_p, *q_args(j), wuv, wo[j],
                         _row(ln_mix_g[l]), _row(ln_mix_b[l]))
        x = moe_block(x, l, True)
    y_prompt = x

    x = x_sample
    conv_sample = []
    for l in range(N_A_LAYERS):
        x, st = _conv_sample(x, state_conv[l], *conv_args(l))
        conv_sample.append(st)
        x = moe_block(x, l, False)
    ckv_s, kpe_s, ckvb_s, kpeb_s = _shared_kv(x.reshape(-1, d), wdkv, _row(s_g_kv), wkr, wkrs, cos_s, sin_s,
                                              tm=512, table_blocks=(bs * ts) // 512)
    cache_kpe_t = jnp.transpose(cache_kpe, (0, 2, 1))
    new_rows = LANES
    cnew =jnp.pad(ckvb_s.reshape(bs, ts, KV_LORA), ((0, 0), (0, new_rows - ts), (0, 0)))
    pnew = jnp.pad(kpeb_s.reshape(bs, ts, ROPE_PAD), ((0, 0), (0, new_rows - ts), (0, 0)))
    for l in range(N_A_LAYERS, DEPTH):
        j = l - N_A_LAYERS
        ql, qp = _q_sample(x, cos_s, sin_s, *q_args(j))
        o_lat = _attn_sample(page_table, ql.reshape(bs, N_HEADS * ts, KV_LORA),
                             qp.reshape(bs, N_HEADS * ts, ROPE_PAD), cnew, pnew, cache_ckv, cache_kpe_t)
        x = _o_sample(o_lat.reshape(bs, N_HEADS, ts, KV_LORA), x, wuv, wo[j],
                      _row(ln_mix_g[l]), _row(ln_mix_b[l]))
        x = moe_block(x, l, False)
    y_sample = x

    return (y_prompt, y_sample, jnp.stack(conv_prompt), jnp.stack(conv_sample),
            ckv_p.reshape(bp, sp, KV_LORA), kpe_p.reshape(bp, sp, QK_ROPE),
            ckv_s.reshape(bs, ts, KV_LORA), kpe_s.reshape(bs, ts, QK_ROPE))
```

```python
import functools

import jax
import jax.numpy as jnp
from jax import lax
from jax.experimental import pallas as pl
from jax.experimental.pallas import tpu as pltpu

D_MODEL = 1024
DEPTH = 4
N_A_LAYERS = DEPTH // 2
CONV_WIDTH = 31
CONV_STATE = CONV_WIDTH - 1
N_HEADS = 8
QK_NOPE = 128
QK_ROPE = 64
V_HEAD = 128
KV_LORA = 256
Q_LORA = 512
ROPE_BASE = 10000.0
SCORE_SCALE = (QK_NOPE + QK_ROPE) ** -0.5
N_EXPERTS = 16
N_GROUPS = 4
EXPERTS_PER_GROUP = N_EXPERTS // N_GROUPS
D_EXPERT = 512
ALPHA = (2 * DEPTH) ** 0.25
LN_EPS = 1e-5
RMS_EPS = 1e-6

LANES = 128
SUBLANES = 8
VMEM_LIMIT_BYTES = 56 * 1024 * 1024

ROPE_PAD = LANES
HIST_ROWS = 32
HIST_SKIP = HIST_ROWS - CONV_STATE

MEMBER_PAIRS = ((0, 1), (0, 2), (0, 3), (1, 2), (1, 3), (2, 3))
N_CLASSES = N_GROUPS * len(MEMBER_PAIRS)
CLASS_ROWS = 32
GATE_PAD = LANES
MOE_TILE = 512

BF16 = jnp.bfloat16
F32 = jnp.float32
NEG_INF = float("-inf")


def _cparams(*sem):
    return pltpu.CompilerParams(dimension_semantics=sem, vmem_limit_bytes=VMEM_LIMIT_BYTES)


def _dot(a, b):
    return jnp.dot(a, b, preferred_element_type=F32)


def _dot_nt(a, b):
    return lax.dot_general(a, b, (((1,), (1,)), ((), ())), preferred_element_type=F32)


def _layer_norm(v, g, b):
    mu = jnp.mean(v, axis=-1, keepdims=True)
    d = v - mu
    var = jnp.mean(d * d, axis=-1, keepdims=True)
    return d * lax.rsqrt(var + LN_EPS) * g + b


def _rms_norm(v, g):
    return v * lax.rsqrt(jnp.mean(v * v, axis=-1, keepdims=True) + RMS_EPS) * g


def _sigmoid(v):
    return 1.0 / (1.0 + jnp.exp(-v))


def _full(shape):
    n = len(shape)
    return pl.BlockSpec(shape, lambda *_: (0,) * n)


def _conv_tail(conv, x, gcn, bcn, wpw2, bpw2, lg, lb):
    z = _layer_norm(conv, gcn, bcn)
    z = z * _sigmoid(z)
    mix = _dot(z.astype(BF16), wpw2) + bpw2
    return _layer_norm(ALPHA * x + mix, lg, lb)


def _conv_prompt_kernel(x_ref, wpw1_ref, bpw1_ref, w8_ref, bdw_ref, gcn_ref, bcn_ref, wpw2_ref, bpw2_ref,
                        lg_ref, lb_ref, y_ref, state_ref, ext_ref, conv_ref, *, ts, parts, row_chunk, lane_chunk):
    s = pl.program_id(1)
    base = ext_ref.at[0]

    @pl.when(s == 0)
    def _():
        base[0:HIST_ROWS, :] = jnp.zeros((HIST_ROWS, D_MODEL), F32)

    @pl.when(s > 0)
    def _():
        base[0:HIST_ROWS, :] = base[ts:ts + HIST_ROWS, :]

    groups = row_chunk // SUBLANES
    part = ts // parts

    def glu(r0):
        h = _dot(x_ref[r0:r0 + part, :].astype(BF16), wpw1_ref[...]) + bpw1_ref[...]
        base[HIST_ROWS + r0:HIST_ROWS + r0 + part, :] = h[:, :D_MODEL] * _sigmoid(h[:, D_MODEL:])

    def conv_rows(r0):
        span = part + HIST_ROWS - SUBLANES
        for j in range(1, SUBLANES):
            ext_ref[j, r0:r0 + span, :] = base[r0 + j:r0 + j + span, :]
        for c0 in range(r0, r0 + part, row_chunk):
            for l0 in range(0, D_MODEL, lane_chunk):
                acc = jnp.broadcast_to(bdw_ref[:, l0:l0 + lane_chunk][None], (groups, SUBLANES, lane_chunk))
                for k in range(CONV_WIDTH):
                    off = HIST_SKIP + k
                    j, a = off % SUBLANES, off // SUBLANES
                    start = c0 + SUBLANES * a
                    blk = ext_ref[j, start:start + row_chunk, l0:l0 + lane_chunk]
                    acc = acc + blk.reshape(groups, SUBLANES, lane_chunk) * w8_ref[k, :, l0:l0 + lane_chunk][None]
                conv_ref[c0:c0 + row_chunk, l0:l0 + lane_chunk] = acc.reshape(row_chunk, lane_chunk)
        y_ref[r0:r0 + part, :] = _conv_tail(conv_ref[r0:r0 + part, :], x_ref[r0:r0 + part, :], gcn_ref[...],
                                            bcn_ref[...], wpw2_ref[...], bpw2_ref[...], lg_ref[...], lb_ref[...])

    for p in range(parts):
        glu(p * part)
    for p in range(parts):
        conv_rows(p * part)

    @pl.when(s == pl.num_programs(1) - 1)
    def _():
        state_ref[...] = base[ts + HIST_SKIP:ts + HIST_ROWS, :]


def _conv_prompt(x, wpw1, bpw1, w8, bdw, gcn, bcn, wpw2, bpw2, lg, lb, *, ts=256):
    b, s, d = x.shape
    kern = functools.partial(_conv_prompt_kernel, ts=ts, parts=2, row_chunk=32, lane_chunk=256)
    return pl.pallas_call(
        kern,
        grid=(b, s // ts),
        in_specs=[
            pl.BlockSpec((None, ts, d), lambda i, j: (i, j, 0)),
            _full(wpw1.shape), _full(bpw1.shape), _full(w8.shape), _full(bdw.shape), _full(gcn.shape),
            _full(bcn.shape), _full(wpw2.shape), _full(bpw2.shape), _full(lg.shape), _full(lb.shape),
        ],
        out_specs=[
            pl.BlockSpec((None, ts, d), lambda i, j: (i, j, 0)),
            pl.BlockSpec((None, CONV_STATE, d), lambda i, j: (i, 0, 0)),
        ],
        out_shape=[
            jax.ShapeDtypeStruct((b, s, d), F32),
            jax.ShapeDtypeStruct((b, CONV_STATE, d), F32),
        ],
        scratch_shapes=[
            pltpu.VMEM((SUBLANES, ts + HIST_ROWS, d), F32),
            pltpu.VMEM((ts, d), F32),
        ],
        compiler_params=_cparams("arbitrary", "arbitrary"),
        name="conv_prompt",
    )(x, wpw1, bpw1, w8, bdw, gcn, bcn, wpw2, bpw2, lg, lb)


def _conv_sample_kernel(x_ref, past_ref, wpw1_ref, bpw1_ref, w8_ref, bdw_ref, gcn_ref, bcn_ref, wpw2_ref,
                        bpw2_ref, lg_ref, lb_ref, y_ref, state_ref, ext_ref, conv_ref, *, bb, t, lane_chunk):
    x = x_ref[...].reshape(bb * t, D_MODEL)
    h = _dot(x.astype(BF16), wpw1_ref[...]) + bpw1_ref[...]
    u = h[:, :D_MODEL] * _sigmoid(h[:, D_MODEL:])
    ext_ref[:, HIST_SKIP:HIST_ROWS, :] = past_ref[...]
    ext_ref[:, HIST_ROWS:HIST_ROWS + t, :] = u.reshape(bb, t, D_MODEL)
    for l0 in range(0, D_MODEL, lane_chunk):
        acc = jnp.broadcast_to(bdw_ref[:, l0:l0 + lane_chunk][None], (bb, t, lane_chunk))
        for k in range(CONV_WIDTH):
            off = HIST_SKIP + k
            acc = acc + ext_ref[:, off:off + t, l0:l0 + lane_chunk] * w8_ref[k, :, l0:l0 + lane_chunk][None]
        conv_ref[:, :, l0:l0 + lane_chunk] = acc
    y = _conv_tail(conv_ref[...].reshape(bb * t, D_MODEL), x, gcn_ref[...], bcn_ref[...], wpw2_ref[...],
                   bpw2_ref[...], lg_ref[...], lb_ref[...])
    y_ref[...] = y.reshape(bb, t, D_MODEL)
    state_ref[...] = ext_ref[:, HIST_SKIP + t:HIST_ROWS + t, :]


def _conv_sample(x, past, wpw1, bpw1, w8, bdw, gcn, bcn, wpw2, bpw2, lg, lb, *, bb=16):
    b, t, d = x.shape
    assert t == SUBLANES
    kern = functools.partial(_conv_sample_kernel, bb=bb, t=t, lane_chunk=256)
    return pl.pallas_call(
        kern,
        grid=(b // bb,),
        in_specs=[
            pl.BlockSpec((bb, t, d), lambda i: (i, 0, 0)),
            pl.BlockSpec((bb, CONV_STATE, d), lambda i: (i, 0, 0)),
            _full(wpw1.shape), _full(bpw1.shape), _full(w8.shape), _full(bdw.shape), _full(gcn.shape),
            _full(bcn.shape), _full(wpw2.shape), _full(bpw2.shape), _full(lg.shape), _full(lb.shape),
        ],
        out_specs=[
            pl.BlockSpec((bb, t, d), lambda i: (i, 0, 0)),
            pl.BlockSpec((bb, CONV_STATE, d), lambda i: (i, 0, 0)),
        ],
        out_shape=[
            jax.ShapeDtypeStruct((b, t, d), F32),
            jax.ShapeDtypeStruct((b, CONV_STATE, d), F32),
        ],
        scratch_shapes=[
            pltpu.VMEM((bb, HIST_ROWS + t, d), F32),
            pltpu.VMEM((bb, t, d), F32),
        ],
        compiler_params=_cparams("arbitrary"),
        name="conv_sample",
    )(x, past, wpw1, bpw1, w8, bdw, gcn, bcn, wpw2, bpw2, lg, lb)


def _router_kernel(x_ref, rwt_ref, rb_ref, gates_ref, cls_ref, pair_ref, cnt_ref):
    x = x_ref[...]
    w = rwt_ref[...]
    x_hi = x.astype(BF16)
    x_lo = (x - x_hi.astype(F32)).astype(BF16)
    w_hi = w.astype(BF16)
    w_lo = (w - w_hi.astype(F32)).astype(BF16)
    by_hi = _dot_nt(jnp.concatenate([w_hi, w_lo], axis=0), x_hi)
    logits = by_hi[:N_EXPERTS] + by_hi[N_EXPERTS:] + _dot_nt(w_hi, x_lo)
    aff = _sigmoid(logits)
    sel = aff + rb_ref[...]
    g = N_GROUPS
    s = [sel[m * g:(m + 1) * g] for m in range(EXPERTS_PER_GROUP)]
    a = [aff[m * g:(m + 1) * g] for m in range(EXPERTS_PER_GROUP)]
    hi01, lo01 = jnp.maximum(s[0], s[1]), jnp.minimum(s[0], s[1])
    hi23, lo23 = jnp.maximum(s[2], s[3]), jnp.minimum(s[2], s[3])
    top1 = jnp.maximum(hi01, hi23)
    top2 = jnp.maximum(jnp.minimum(hi01, hi23), jnp.maximum(lo01, lo23))
    score = top1 + top2
    best = score[0:1]
    best_idx = jnp.zeros(best.shape, jnp.int32)
    for gi in range(1, g):
        better = score[gi:gi + 1] > best
        best = jnp.where(better, score[gi:gi + 1], best)
        best_idx = jnp.where(better, gi, best_idx)
    in_group = lax.broadcasted_iota(jnp.int32, score.shape, 0) == best_idx
    picked, chosen = [], []
    for m in range(EXPERTS_PER_GROUP):
        rank = jnp.zeros(score.shape, jnp.int32)
        for j in range(EXPERTS_PER_GROUP):
            if j == m:
                continue
            ahead = (s[j] >= s[m]) if j < m else (s[j] > s[m])
            rank = rank + ahead.astype(jnp.int32)
        keep = in_group & (rank < 2)
        picked.append(jnp.where(keep, a[m], 0.0))
        chosen.append(jnp.where(keep, 1.0, 0.0))
    p = [jnp.sum(v, axis=0, keepdims=True) for v in picked]
    on = [jnp.sum(v, axis=0, keepdims=True) > 0.5 for v in chosen]
    inv = 1.0 / (p[0] + p[1] + p[2] + p[3])
    for m in range(EXPERTS_PER_GROUP):
        gates_ref[m * g:(m + 1) * g, :] = picked[m] * inv
    lo = jnp.where(on[0], 0, jnp.where(on[1], 1, 2))
    hi = jnp.where(on[3], 3, jnp.where(on[2], 2, 1))
    p_lo = jnp.where(on[0], p[0], jnp.where(on[1], p[1], p[2]))
    p_hi = jnp.where(on[3], p[3], jnp.where(on[2], p[2], p[1]))
    pair = jnp.where(lo == 0, hi - 1, jnp.where(lo == 1, hi + 1, len(MEMBER_PAIRS) - 1))
    cls = best_idx * len(MEMBER_PAIRS) + pair
    cls_ref[...] = cls

    @pl.when(pl.program_id(0) == 0)
    def _():
        cnt_ref[...] = jnp.zeros(cnt_ref.shape, F32)

    onehot = lax.broadcasted_iota(jnp.int32, (CLASS_ROWS, cls.shape[1]), 0) == cls
    cnt_ref[...] += jnp.sum(jnp.where(onehot, 1.0, 0.0), axis=1, keepdims=True)
    pair_ref[...] = jnp.zeros(pair_ref.shape, F32)
    pair_ref[0:1, :] = p_lo * inv
    pair_ref[1:2, :] = p_hi * inv


def _router(x2d, rwt, rb, *, tm=1024):
    t, d = x2d.shape
    return pl.pallas_call(
        _router_kernel,
        grid=(t // tm,),
        in_specs=[pl.BlockSpec((tm, d), lambda i: (i, 0)), _full(rwt.shape), _full(rb.shape)],
        out_specs=[
            pl.BlockSpec((N_EXPERTS, tm), lambda i: (0, i)),
            pl.BlockSpec((1, tm), lambda i: (0, i)),
            pl.BlockSpec((SUBLANES, tm), lambda i: (0, i)),
            pl.BlockSpec((CLASS_ROWS, LANES), lambda i: (0, 0)),
        ],
        out_shape=[
            jax.ShapeDtypeStruct((N_EXPERTS, t), F32),
            jax.ShapeDtypeStruct((1, t), jnp.int32),
            jax.ShapeDtypeStruct((SUBLANES, t), F32),
            jax.ShapeDtypeStruct((CLASS_ROWS, LANES), F32),
        ],
        compiler_params=_cparams("arbitrary"),
        name="router",
    )(x2d, rwt, rb)


def _positions_kernel(cls_ref, cnt_ref, pos_ref, tcls_ref, nt_ref, start_ref, run_ref, *, tb, tm):
    i = pl.program_id(0)
    shift = tm.bit_length() - 1
    onehot = lax.broadcasted_iota(jnp.int32, (CLASS_ROWS, tb), 0) == cls_ref[...]
    ohf = jnp.where(onehot, 1.0, 0.0)

    @pl.when(i == 0)
    def _():
        cnt = cnt_ref[...].astype(jnp.int32)
        padded = (((cnt + (tm - 1)) >> shift) << shift).astype(F32)
        r = lax.broadcasted_iota(jnp.int32, (CLASS_ROWS, CLASS_ROWS), 0)
        c = lax.broadcasted_iota(jnp.int32, (CLASS_ROWS, CLASS_ROWS), 1)
        start = jnp.dot(jnp.where(c < r, 1.0, 0.0), padded, preferred_element_type=F32,
                        precision=lax.Precision.HIGHEST)
        start_ref[...] = start
        run_ref[...] = jnp.zeros(run_ref.shape, F32)
        tile_start = (lax.broadcasted_iota(jnp.int32, (CLASS_ROWS, LANES), 1) << shift).astype(F32)
        real = lax.broadcasted_iota(jnp.int32, (CLASS_ROWS, LANES), 0) < N_CLASSES
        below = jnp.where(real, jnp.where(start <= tile_start, 1.0, 0.0), 0.0)
        tcls_ref[...] = jnp.sum(below, axis=0, keepdims=True).astype(jnp.int32) - 1
        nt_ref[...] = jnp.sum(padded, axis=0, keepdims=True).astype(jnp.int32) >> shift

    rr = lax.broadcasted_iota(jnp.int32, (tb, tb), 0)
    cc = lax.broadcasted_iota(jnp.int32, (tb, tb), 1)
    upper = jnp.where(rr <= cc, 1.0, 0.0).astype(BF16)
    cum = _dot(ohf.astype(BF16), upper)
    base = run_ref[:, 0:1] + start_ref[:, 0:1] - 1.0
    pos_ref[...] = jnp.sum(ohf * (cum + base), axis=0, keepdims=True).astype(jnp.int32)
    run_ref[...] += cum[:, tb - 1:tb]


def _positions(cls, cnt, *, tb=512, tm=MOE_TILE):
    t = cls.shape[1]
    assert tm & (tm - 1) == 0 and (t + N_CLASSES * tm) // tm <= LANES
    kern = functools.partial(_positions_kernel, tb=tb, tm=tm)
    return pl.pallas_call(
        kern,
        grid=(t // tb,),
        in_specs=[pl.BlockSpec((1, tb), lambda i: (0, i)), _full(cnt.shape)],
        out_specs=[
            pl.BlockSpec((1, tb), lambda i: (0, i)),
            pl.BlockSpec((1, LANES), lambda i: (0, 0)),
            pl.BlockSpec((1, LANES), lambda i: (0, 0)),
        ],
        out_shape=[
            jax.ShapeDtypeStruct((1, t), jnp.int32),
            jax.ShapeDtypeStruct((1, LANES), jnp.int32),
            jax.ShapeDtypeStruct((1, LANES), jnp.int32),
        ],
        scratch_shapes=[pltpu.VMEM((CLASS_ROWS, LANES), F32)] * 2,
        compiler_params=_cparams("arbitrary"),
        name="positions",
    )(cls, cnt)


def _staged_row(ref, c, u):
    return ref.at[c, pl.ds(u, 1), :]


def _hbm_row(ref, r):
    return ref.at[r >> (SUBLANES.bit_length() - 1), pl.ds(r & (SUBLANES - 1), 1), :]


def _row_scatter_kernel(pos_ref, x_ref, pair_ref, init_hbm, dst_hbm, aug_ref, gcol_ref, sem, *, tb):
    del init_hbm
    groups = tb // SUBLANES
    aug_ref[:, :, :D_MODEL] = x_ref[...]
    gcol_ref[...] = jnp.zeros(gcol_ref.shape, F32)
    gcol_ref[0:SUBLANES, :] = pair_ref[...]
    aug_ref[:, :, D_MODEL:] = gcol_ref[...].T.reshape(groups, SUBLANES, GATE_PAD)

    def issue(c, carry):
        for u in range(SUBLANES):
            dst = _hbm_row(dst_hbm, pos_ref[0, c * SUBLANES + u])
            pltpu.make_async_copy(_staged_row(aug_ref, c, u), dst, sem).start(priority=u % 2)
        return carry

    lax.fori_loop(0, groups, issue, 0)
    pltpu.make_async_copy(aug_ref, dst_hbm.at[pl.ds(0, groups)], sem).wait()


def _row_scatter(pos3, x2d, pair, init):
    nb, _, tb = pos3.shape
    t, d = x2d.shape
    rows, da = init.shape
    kern = functools.partial(_row_scatter_kernel, tb=tb)
    out = pl.pallas_call(
        kern,
        grid=(nb,),
        in_specs=[
            pl.BlockSpec((None, 1, tb), lambda i: (i, 0, 0), memory_space=pltpu.SMEM),
            pl.BlockSpec((tb // SUBLANES, SUBLANES, d), lambda i: (i, 0, 0)),
            pl.BlockSpec((SUBLANES, tb), lambda i: (0, i)),
            pl.BlockSpec(memory_space=pl.ANY),
        ],
        out_specs=pl.BlockSpec(memory_space=pl.ANY),
        out_shape=jax.ShapeDtypeStruct((rows // SUBLANES, SUBLANES, da), init.dtype),
        scratch_shapes=[
            pltpu.VMEM((tb // SUBLANES, SUBLANES, da), F32),
            pltpu.VMEM((GATE_PAD, tb), F32),
            pltpu.SemaphoreType.DMA,
        ],
        input_output_aliases={3: 0},
        compiler_params=_cparams("arbitrary"),
        name="row_scatter",
    )(pos3, x2d.reshape(t // SUBLANES, SUBLANES, d), pair, init.reshape(rows // SUBLANES, SUBLANES, da))
    return out.reshape(rows, da)


def _row_gather_kernel(pos_ref, src_hbm, y_ref, sem, *, tb):
    groups = tb // SUBLANES

    def issue(c, carry):
        for u in range(SUBLANES):
            src = _hbm_row(src_hbm, pos_ref[0, c * SUBLANES + u])
            pltpu.make_async_copy(src, _staged_row(y_ref, c, u), sem).start(priority=u % 2)
        return carry

    lax.fori_loop(0, groups, issue, 0)
    pltpu.make_async_copy(src_hbm.at[pl.ds(0, groups)], y_ref, sem).wait()


def _row_gather(pos3, src):
    nb, _, tb = pos3.shape
    rows, d = src.shape
    kern = functools.partial(_row_gather_kernel, tb=tb)
    out = pl.pallas_call(
        kern,
        grid=(nb,),
        in_specs=[
            pl.BlockSpec((None, 1, tb), lambda i: (i, 0, 0), memory_space=pltpu.SMEM),
            pl.BlockSpec(memory_space=pl.ANY),
        ],
        out_specs=pl.BlockSpec((tb // SUBLANES, SUBLANES, d), lambda i: (i, 0, 0)),
        out_shape=jax.ShapeDtypeStruct((nb * tb // SUBLANES, SUBLANES, d), src.dtype),
        scratch_shapes=[pltpu.SemaphoreType.DMA],
        compiler_params=_cparams("arbitrary"),
        name="row_gather",
    )(pos3, src.reshape(rows // SUBLANES, SUBLANES, d))
    return out.reshape(nb * tb, d)


def _row_gather_kv_kernel(pos_ref, src_hbm, wdkv_ref, gkv_ref, wkr_ref, wkrs_ref, cos_ref, sin_ref,
                          y_ref, ckv_ref, kpe_ref, ckvb_ref, kpeb_ref, sem, *, tb):
    _row_gather_kernel(pos_ref, src_hbm, y_ref, sem, tb=tb)
    xb = y_ref[...].reshape(tb, D_MODEL).astype(BF16)
    ckv = _rms_norm(_dot(xb, wdkv_ref[...]), gkv_ref[...])
    kpe = _dot(xb, wkr_ref[...]) * cos_ref[...] + _dot(xb, wkrs_ref[...]) * sin_ref[...]
    ckv_ref[...] = ckv
    kpe_ref[...] = kpe[:, :QK_ROPE]
    ckvb_ref[...] = ckv.astype(BF16)
    kpeb_ref[...] = kpe.astype(BF16)


def _row_gather_kv(pos3, src, wdkv, gkv, wkr, wkrs, cos, sin):
    nb, _, tb = pos3.shape
    rows, d = src.shape
    t = nb * tb
    table_blocks = cos.shape[0] // tb
    kern = functools.partial(_row_gather_kv_kernel, tb=tb)
    tab = pl.BlockSpec((tb, ROPE_PAD), lambda i: (i % table_blocks, 0))
    outs = pl.pallas_call(
        kern,
        grid=(nb,),
        in_specs=[
            pl.BlockSpec((None, 1, tb), lambda i: (i, 0, 0), memory_space=pltpu.SMEM),
            pl.BlockSpec(memory_space=pl.ANY),
            _full(wdkv.shape), _full(gkv.shape), _full(wkr.shape), _full(wkrs.shape), tab, tab,
        ],
        out_specs=[
            pl.BlockSpec((tb // SUBLANES, SUBLANES, d), lambda i: (i, 0, 0)),
            pl.BlockSpec((tb, KV_LORA), lambda i: (i, 0)),
            pl.BlockSpec((tb, QK_ROPE), lambda i: (i, 0)),
            pl.BlockSpec((tb, KV_LORA), lambda i: (i, 0)),
            pl.BlockSpec((tb, ROPE_PAD), lambda i: (i, 0)),
        ],
        out_shape=[
            jax.ShapeDtypeStruct((t // SUBLANES, SUBLANES, d), src.dtype),
            jax.ShapeDtypeStruct((t, KV_LORA), F32),
            jax.ShapeDtypeStruct((t, QK_ROPE), F32),
            jax.ShapeDtypeStruct((t, KV_LORA), BF16),
            jax.ShapeDtypeStruct((t, ROPE_PAD), BF16),
        ],
        scratch_shapes=[pltpu.SemaphoreType.DMA],
        compiler_params=_cparams("arbitrary"),
        name="row_gather_kv",
    )(pos3, src.reshape(rows // SUBLANES, SUBLANES, d), wdkv, gkv, wkr, wkrs, cos, sin)
    return (outs[0].reshape(t, d),) + tuple(outs[1:])


def _moe_sorted_kernel(ea_ref, eb_ref, nt_ref, xs_ref, w1a_ref, w3a_ref, w2a_ref, w1b_ref, w3b_ref, w2b_ref,
                       lg_ref, lb_ref, y_ref, *wb_refs):
    i = pl.program_id(0)
    used = i < nt_ref[0]
    prev = jnp.maximum(i - 1, 0)
    new_pair = (i == 0) | (ea_ref[i] != ea_ref[prev]) | (eb_ref[i] != eb_ref[prev])

    @pl.when(jnp.logical_not(used))
    def _():
        y_ref[...] = jnp.zeros(y_ref.shape, F32)

    @pl.when(used & new_pair)
    def _():
        for src, dst in zip((w1a_ref, w3a_ref, w2a_ref, w1b_ref, w3b_ref, w2b_ref), wb_refs):
            dst[...] = src[...].astype(BF16)

    @pl.when(used)
    def _():
        x = xs_ref[:, :D_MODEL]
        xb = x.astype(BF16)

        def ffn(w1_ref, w3_ref, w2_ref):
            h1 = _dot(xb, w1_ref[...])
            h3 = _dot(xb, w3_ref[...])
            return _dot((h1 * _sigmoid(h1) * h3).astype(BF16), w2_ref[...])

        out = xs_ref[:, D_MODEL:D_MODEL + 1] * ffn(*wb_refs[:3])
        out = out + xs_ref[:, D_MODEL + 1:D_MODEL + 2] * ffn(*wb_refs[3:])
        y_ref[...] = _layer_norm(ALPHA * x + out, lg_ref[...], lb_ref[...])


def _moe_sorted(ea, eb, nt, xs, layer, w1, w3, w2, lg, lb, *, tm=MOE_TILE):
    p, da = xs.shape
    _, _, d, f = w1.shape

    def row_block(i, ea, eb, nt):
        return (i, 0)

    def wa(i, ea, eb, nt):
        return (layer, ea[i], 0, 0)

    def wb(i, ea, eb, nt):
        return (layer, eb[i], 0, 0)

    grid_spec = pltpu.PrefetchScalarGridSpec(
        num_scalar_prefetch=3,
        grid=(p // tm,),
        in_specs=[
            pl.BlockSpec((tm, da), row_block),
            pl.BlockSpec((None, None, d, f), wa), pl.BlockSpec((None, None, d, f), wa),
            pl.BlockSpec((None, None, f, d), wa),
            pl.BlockSpec((None, None, d, f), wb), pl.BlockSpec((None, None, d, f), wb),
            pl.BlockSpec((None, None, f, d), wb),
            pl.BlockSpec(lg.shape, lambda i, ea, eb, nt: (0, 0)),
            pl.BlockSpec(lb.shape, lambda i, ea, eb, nt: (0, 0)),
        ],
        out_specs=pl.BlockSpec((tm, d), row_block),
        scratch_shapes=[pltpu.VMEM((d, f), BF16), pltpu.VMEM((d, f), BF16), pltpu.VMEM((f, d), BF16)] * 2,
    )
    return pl.pallas_call(
        _moe_sorted_kernel,
        grid_spec=grid_spec,
        out_shape=jax.ShapeDtypeStruct((p, d), F32),
        compiler_params=_cparams("arbitrary"),
        name="moe_sorted",
    )(ea, eb, nt, xs, w1, w3, w2, w1, w3, w2, lg, lb)


def _moe_kernel(x_ref, gates_ref, w1_ref, w3_ref, w2_ref, lg_ref, lb_ref, y_ref, xb_ref, acc_ref):
    e = pl.program_id(1)

    @pl.when(e == 0)
    def _():
        xb_ref[...] = x_ref[...].astype(BF16)
        acc_ref[...] = jnp.zeros(acc_ref.shape, F32)

    xb = xb_ref[...]
    h1 = _dot(xb, w1_ref[...].astype(BF16))
    h3 = _dot(xb, w3_ref[...].astype(BF16))
    h = (h1 * _sigmoid(h1) * h3).astype(BF16)
    lane = lax.broadcasted_iota(jnp.int32, gates_ref.shape, 1)
    gate = jnp.sum(jnp.where(lane == e, gates_ref[...], 0.0), axis=1, keepdims=True)
    acc_ref[...] += gate * _dot(h, w2_ref[...].astype(BF16))

    @pl.when(e == pl.num_programs(1) - 1)
    def _():
        y_ref[...] = _layer_norm(ALPHA * x_ref[...] + acc_ref[...], lg_ref[...], lb_ref[...])


def _moe(x2d, gates, layer, w1, w3, w2, lg, lb, *, tm=1024):
    t, d = x2d.shape
    _, e, _, f = w1.shape
    return pl.pallas_call(
        _moe_kernel,
        grid=(t // tm, e),
        in_specs=[
            pl.BlockSpec((tm, d), lambda i, j: (i, 0)),
            pl.BlockSpec((tm, e), lambda i, j: (i, 0)),
            pl.BlockSpec((None, None, d, f), lambda i, j: (layer, j, 0, 0)),
            pl.BlockSpec((None, None, d, f), lambda i, j: (layer, j, 0, 0)),
            pl.BlockSpec((None, None, f, d), lambda i, j: (layer, j, 0, 0)),
            _full(lg.shape), _full(lb.shape),
        ],
        out_specs=pl.BlockSpec((tm, d), lambda i, j: (i, 0)),
        out_shape=jax.ShapeDtypeStruct((t, d), F32),
        scratch_shapes=[pltpu.VMEM((tm, d), BF16), pltpu.VMEM((tm, d), F32)],
        compiler_params=_cparams("arbitrary", "arbitrary"),
        name="moe",
    )(x2d, gates, w1, w3, w2, lg, lb)


def _shared_kv_kernel(x_ref, wdkv_ref, gkv_ref, wkr_ref, wkrs_ref, cos_ref, sin_ref,
                      ckv_ref, kpe_ref, ckvb_ref, kpeb_ref):
    xb = x_ref[...].astype(BF16)
    ckv = _rms_norm(_dot(xb, wdkv_ref[...]), gkv_ref[...])
    kpe = _dot(xb, wkr_ref[...]) * cos_ref[...] + _dot(xb, wkrs_ref[...]) * sin_ref[...]
    ckv_ref[...] = ckv
    kpe_ref[...] = kpe[:, :QK_ROPE]
    ckvb_ref[...] = ckv.astype(BF16)
    kpeb_ref[...] = kpe.astype(BF16)


def _shared_kv(x2d, wdkv, gkv, wkr, wkrs, cos, sin, *, tm, table_blocks):
    t, d = x2d.shape
    tab = pl.BlockSpec((tm, ROPE_PAD), lambda i: (i % table_blocks, 0))
    return pl.pallas_call(
        _shared_kv_kernel,
        grid=(t // tm,),
        in_specs=[pl.BlockSpec((tm, d), lambda i: (i, 0)), _full(wdkv.shape), _full(gkv.shape),
                  _full(wkr.shape), _full(wkrs.shape), tab, tab],
        out_specs=[
            pl.BlockSpec((tm, KV_LORA), lambda i: (i, 0)),
            pl.BlockSpec((tm, QK_ROPE), lambda i: (i, 0)),
            pl.BlockSpec((tm, KV_LORA), lambda i: (i, 0)),
            pl.BlockSpec((tm, ROPE_PAD), lambda i: (i, 0)),
        ],
        out_shape=[
            jax.ShapeDtypeStruct((t, KV_LORA), F32),
            jax.ShapeDtypeStruct((t, QK_ROPE), F32),
            jax.ShapeDtypeStruct((t, KV_LORA), BF16),
            jax.ShapeDtypeStruct((t, ROPE_PAD), BF16),
        ],
        compiler_params=_cparams("arbitrary"),
        name="shared_kv",
    )(x2d, wdkv, gkv, wkr, wkrs, cos, sin)


def _project_queries(x, cos, sin, wdq, gq, wuqn, wuqp, wuqps, wukt_ref, store):
    cq = _rms_norm(_dot(x.astype(BF16), wdq), gq).astype(BF16)
    qn = _dot(cq, wuqn)
    qp = _dot(cq, wuqp)
    qps = _dot(cq, wuqps)
    cos, sin = cos * SCORE_SCALE, sin * SCORE_SCALE
    for h in range(N_HEADS):
        lat = _dot(qn[:, h * QK_NOPE:(h + 1) * QK_NOPE].astype(BF16), wukt_ref[h]) * SCORE_SCALE
        sl = slice(h * ROPE_PAD, (h + 1) * ROPE_PAD)
        pe = qp[:, sl] * cos + qps[:, sl] * sin
        store(h, lat.astype(BF16), pe.astype(BF16))


def _project_output(o_heads, x, wuv_ref, wo, lg, lb):
    o = jnp.concatenate([_dot(o_heads[h].astype(BF16), wuv_ref[h]) for h in range(N_HEADS)], axis=-1)
    mix = _dot(o.astype(BF16), wo)
    return _layer_norm(ALPHA * x + mix, lg, lb)


def _attn_prompt_kernel(x_ref, ckv_ref, kpe_ref, cos_ref, sin_ref, wdq_ref, gq_ref, wuqn_ref, wuqp_ref,
                        wuqps_ref, wukt_ref, wuv_ref, wo_ref, lg_ref, lb_ref, y_ref,
                        ql_ref, qp_ref, m_ref, l_ref, acc_ref, *, tq, tk, group_heads):
    i = pl.program_id(1)
    x = x_ref[...]

    def store(h, lat, pe):
        ql_ref[h * tq:(h + 1) * tq, :] = lat
        qp_ref[h * tq:(h + 1) * tq, :] = pe

    _project_queries(x, cos_ref[...], sin_ref[...], wdq_ref[...], gq_ref[...], wuqn_ref[...], wuqp_ref[...],
                     wuqps_ref[...], wukt_ref, store)

    rows = N_HEADS * tq
    m_ref[...] = jnp.full((rows, 1), NEG_INF, F32)
    l_ref[...] = jnp.zeros((rows, 1), F32)
    acc_ref[...] = jnp.zeros((rows, KV_LORA), F32)

    group = group_heads * tq

    def block(kb, masked):
        k0 = pl.multiple_of(kb * tk, tk)
        kc = ckv_ref[pl.ds(k0, tk), :]
        kp = kpe_ref[pl.ds(k0, tk), :]
        for r0 in range(0, rows, group):
            rs = slice(r0, r0 + group)
            s = _dot_nt(ql_ref[rs, :], kc) + _dot_nt(qp_ref[rs, :], kp)
            if masked:
                r = lax.broadcasted_iota(jnp.int32, (group, tk), 0) & (tq - 1)
                c = lax.broadcasted_iota(jnp.int32, (group, tk), 1)
                s = jnp.where(k0 + c <= i * tq + r, s, NEG_INF)
            m_old = m_ref[rs, :]
            m_new = jnp.maximum(m_old, jnp.max(s, axis=-1, keepdims=True))
            p = jnp.exp(s - m_new)
            scale = jnp.exp(m_old - m_new)
            l_ref[rs, :] = scale * l_ref[rs, :] + jnp.sum(p, axis=-1, keepdims=True)
            acc_ref[rs, :] = scale * acc_ref[rs, :] + _dot(p.astype(BF16), kc)
            m_ref[rs, :] = m_new

    n_full = (i * tq) // tk

    def body(kb, carry):
        block(kb, False)
        return carry

    lax.fori_loop(0, n_full, body, 0)
    block(n_full, True)

    inv = 1.0 / l_ref[...]
    o_heads = [acc_ref[h * tq:(h + 1) * tq, :] * inv[h * tq:(h + 1) * tq] for h in range(N_HEADS)]
    y_ref[...] = _project_output(o_heads, x, wuv_ref, wo_ref[...], lg_ref[...], lb_ref[...])


def _attn_prompt(x, ckvb, kpeb, cos, sin, wdq, gq, wuqn, wuqp, wuqps, wukt, wuv, wo, lg, lb, *, tq=256, tk=512,
                 group_heads=2):
    b, s, d = x.shape
    assert tk % tq == 0 and s % tk == 0 and tq & (tq - 1) == 0 and N_HEADS % group_heads == 0
    kern = functools.partial(_attn_prompt_kernel, tq=tq, tk=tk, group_heads=group_heads)
    rows = N_HEADS * tq
    return pl.pallas_call(
        kern,
        grid=(b, s // tq),
        in_specs=[
            pl.BlockSpec((None, tq, d), lambda i, j: (i, j, 0)),
            pl.BlockSpec((None, s, KV_LORA), lambda i, j: (i, 0, 0)),
            pl.BlockSpec((None, s, ROPE_PAD), lambda i, j: (i, 0, 0)),
            pl.BlockSpec((tq, ROPE_PAD), lambda i, j: (j, 0)),
            pl.BlockSpec((tq, ROPE_PAD), lambda i, j: (j, 0)),
            _full(wdq.shape), _full(gq.shape), _full(wuqn.shape), _full(wuqp.shape), _full(wuqps.shape),
            _full(wukt.shape), _full(wuv.shape), _full(wo.shape), _full(lg.shape), _full(lb.shape),
        ],
        out_specs=pl.BlockSpec((None, tq, d), lambda i, j: (i, j, 0)),
        out_shape=jax.ShapeDtypeStruct((b, s, d), F32),
        scratch_shapes=[
            pltpu.VMEM((rows, KV_LORA), BF16),
            pltpu.VMEM((rows, ROPE_PAD), BF16),
            pltpu.VMEM((rows, 1), F32),
            pltpu.VMEM((rows, 1), F32),
            pltpu.VMEM((rows, KV_LORA), F32),
        ],
        compiler_params=_cparams("arbitrary", "arbitrary"),
        name="attn_prompt",
    )(x, ckvb, kpeb, cos, sin, wdq, gq, wuqn, wuqp, wuqps, wukt, wuv, wo, lg, lb)


def _q_sample_kernel(x_ref, cos_ref, sin_ref, wdq_ref, gq_ref, wuqn_ref, wuqp_ref, wuqps_ref, wukt_ref,
                     ql_ref, qp_ref, *, bb, t):
    x = x_ref[...].reshape(bb * t, D_MODEL)

    def store(h, lat, pe):
        ql_ref[:, h] = lat.reshape(bb, t, KV_LORA)
        qp_ref[:, h] = pe.reshape(bb, t, ROPE_PAD)

    _project_queries(x, cos_ref[...], sin_ref[...], wdq_ref[...], gq_ref[...], wuqn_ref[...], wuqp_ref[...],
                     wuqps_ref[...], wukt_ref, store)


def _q_sample(x, cos, sin, wdq, gq, wuqn, wuqp, wuqps, wukt, *, bb=32):
    b, t, d = x.shape
    kern = functools.partial(_q_sample_kernel, bb=bb, t=t)
    return pl.pallas_call(
        kern,
        grid=(b // bb,),
        in_specs=[
            pl.BlockSpec((bb, t, d), lambda i: (i, 0, 0)),
            pl.BlockSpec((bb * t, ROPE_PAD), lambda i: (i, 0)),
            pl.BlockSpec((bb * t, ROPE_PAD), lambda i: (i, 0)),
            _full(wdq.shape), _full(gq.shape), _full(wuqn.shape), _full(wuqp.shape), _full(wuqps.shape),
            _full(wukt.shape),
        ],
        out_specs=[
            pl.BlockSpec((bb, N_HEADS, t, KV_LORA), lambda i: (i, 0, 0, 0)),
            pl.BlockSpec((bb, N_HEADS, t, ROPE_PAD), lambda i: (i, 0, 0, 0)),
        ],
        out_shape=[
            jax.ShapeDtypeStruct((b, N_HEADS, t, KV_LORA), BF16),
            jax.ShapeDtypeStruct((b, N_HEADS, t, ROPE_PAD), BF16),
        ],
        compiler_params=_cparams("arbitrary"),
        name="q_sample",
    )(x, cos, sin, wdq, gq, wuqn, wuqp, wuqps, wukt)


def _attn_sample_kernel(pt_ref, ql_ref, qp_ref, cnew_ref, pnew_ref, ckv_hbm, kpe_hbm, o_ref,
                        cbuf_ref, pbuf_ref, kc_ref, kp_ref, sem, *, pages, page, t):
    b = pl.program_id(0)
    nb = pl.num_programs(0)
    slot = b % 2
    rows = ql_ref.shape[0]

    def fetch(seq, into):
        for p in range(pages):
            src = pt_ref[seq * pages + p]
            pltpu.make_async_copy(ckv_hbm.at[src], cbuf_ref.at[into, p], sem.at[0, into]).start()
            pltpu.make_async_copy(kpe_hbm.at[src], pbuf_ref.at[into, p], sem.at[1, into]).start()

    @pl.when(b == 0)
    def _():
        fetch(0, 0)

    @pl.when(b + 1 < nb)
    def _():
        fetch(b + 1, 1 - slot)

    pltpu.make_async_copy(ckv_hbm.at[pl.ds(0, pages)], cbuf_ref.at[slot], sem.at[0, slot]).wait()
    pltpu.make_async_copy(kpe_hbm.at[pl.ds(0, pages)], pbuf_ref.at[slot], sem.at[1, slot]).wait()

    for p in range(pages):
        kc_ref[p * page:(p + 1) * page, :] = cbuf_ref[slot, p].astype(BF16)
        kp_ref[:, p * page:(p + 1) * page] = pbuf_ref[slot, p].astype(BF16)

    ql = ql_ref[...]
    qp = qp_ref[...]
    kc = kc_ref[...]
    cn = cnew_ref[...]
    s_past = _dot_nt(ql, kc) + _dot(qp[:, :QK_ROPE], kp_ref[...])
    s_new = _dot_nt(ql, cn) + _dot_nt(qp, pnew_ref[...])
    n = cn.shape[0]
    qpos = lax.broadcasted_iota(jnp.int32, (rows, n), 0) & (t - 1)
    kpos = lax.broadcasted_iota(jnp.int32, (rows, n), 1)
    s_new = jnp.where(kpos <= qpos, s_new, NEG_INF)
    m = jnp.maximum(jnp.max(s_past, axis=-1, keepdims=True), jnp.max(s_new, axis=-1, keepdims=True))
    p_past = jnp.exp(s_past - m)
    p_new = jnp.exp(s_new - m)
    l = jnp.sum(p_past, axis=-1, keepdims=True) + jnp.sum(p_new, axis=-1, keepdims=True)
    o_ref[...] = (_dot(p_past.astype(BF16), kc) + _dot(p_new.astype(BF16), cn)) * (1.0 / l)


def _attn_sample(page_table, ql, qp, cnew, pnew, cache_ckv, cache_kpe_t):
    b, rows, _ = ql.shape
    pages = page_table.shape[1]
    page = cache_ckv.shape[1]
    t = rows // N_HEADS
    n_new = cnew.shape[1]
    kern = functools.partial(_attn_sample_kernel, pages=pages, page=page, t=t)
    grid_spec = pltpu.PrefetchScalarGridSpec(
        num_scalar_prefetch=1,
        grid=(b,),
        in_specs=[
            pl.BlockSpec((None, rows, KV_LORA), lambda i, pt: (i, 0, 0)),
            pl.BlockSpec((None, rows, ROPE_PAD), lambda i, pt: (i, 0, 0)),
            pl.BlockSpec((None, n_new, KV_LORA), lambda i, pt: (i, 0, 0)),
            pl.BlockSpec((None, n_new, ROPE_PAD), lambda i, pt: (i, 0, 0)),
            pl.BlockSpec(memory_space=pl.ANY),
            pl.BlockSpec(memory_space=pl.ANY),
        ],
        out_specs=pl.BlockSpec((None, rows, KV_LORA), lambda i, pt: (i, 0, 0)),
        scratch_shapes=[
            pltpu.VMEM((2, pages, page, KV_LORA), F32),
            pltpu.VMEM((2, pages, QK_ROPE, page), F32),
            pltpu.VMEM((pages * page, KV_LORA), BF16),
            pltpu.VMEM((QK_ROPE, pages * page), BF16),
            pltpu.SemaphoreType.DMA((2, 2)),
        ],
    )
    return pl.pallas_call(
        kern,
        grid_spec=grid_spec,
        out_shape=jax.ShapeDtypeStruct((b, rows, KV_LORA), F32),
        compiler_params=_cparams("arbitrary"),
        name="attn_sample",
    )(page_table.reshape(-1), ql, qp, cnew, pnew, cache_ckv, cache_kpe_t)


def _o_sample_kernel(o_ref, x_ref, wuv_ref, wo_ref, lg_ref, lb_ref, y_ref, *, bb, t):
    x = x_ref[...].reshape(bb * t, D_MODEL)
    o_heads = [o_ref[:, h].reshape(bb * t, KV_LORA) for h in range(N_HEADS)]
    y_ref[...] = _project_output(o_heads, x, wuv_ref, wo_ref[...], lg_ref[...], lb_ref[...]).reshape(bb, t, D_MODEL)


def _o_sample(o_lat, x, wuv, wo, lg, lb, *, bb=32):
    b, t, d = x.shape
    kern = functools.partial(_o_sample_kernel, bb=bb, t=t)
    return pl.pallas_call(
        kern,
        grid=(b // bb,),
        in_specs=[
            pl.BlockSpec((bb, N_HEADS, t, KV_LORA), lambda i: (i, 0, 0, 0)),
            pl.BlockSpec((bb, t, d), lambda i: (i, 0, 0)),
            _full(wuv.shape), _full(wo.shape), _full(lg.shape), _full(lb.shape),
        ],
        out_specs=pl.BlockSpec((bb, t, d), lambda i: (i, 0, 0)),
        out_shape=jax.ShapeDtypeStruct((b, t, d), F32),
        compiler_params=_cparams("arbitrary"),
        name="o_sample",
    )(o_lat, x, wuv, wo, lg, lb)


def _rope_tables(pos):
    half = QK_ROPE // 2
    inv_freq = ROPE_BASE ** (-jnp.arange(half, dtype=F32) / half)
    ang = pos.astype(F32)[:, None] * inv_freq[None, :]
    cos, sin = jnp.cos(ang), jnp.sin(ang)
    pad = jnp.zeros((pos.shape[0], ROPE_PAD - QK_ROPE), F32)
    return (jnp.concatenate([cos, cos, pad], axis=-1), jnp.concatenate([-sin, sin, pad], axis=-1))


def _swap_halves(w):
    half = QK_ROPE // 2
    return jnp.concatenate([w[..., half:], w[..., :half]], axis=-1)


def _pad_rope(w):
    return jnp.pad(w, [(0, 0)] * (w.ndim - 1) + [(0, ROPE_PAD - QK_ROPE)])


def _row(v):
    return v.reshape(1, -1)


def kernel(x_prompt, x_sample, state_conv, cache_ckv, cache_kpe, page_table, a_w_pw1, a_b_pw1, a_w_dw, a_b_dw, a_g_cn, a_b_cn, a_w_pw2, a_b_pw2, ln_mix_g, ln_mix_b, ln_ffn_g, ln_ffn_b, b_w_dq, b_g_q, b_w_uq, b_w_o, s_w_dkv, s_g_kv, s_w_kr, s_w_uk, s_w_uv, r_w, r_b, e_w1, e_w3, e_w2):
    bp, sp, d = x_prompt.shape
    bs, ts, _ = x_sample.shape
    past_len = page_table.shape[1] * cache_ckv.shape[1]

    a_w_pw1b, a_w_pw2b = a_w_pw1.astype(BF16), a_w_pw2.astype(BF16)
    w8 = jnp.broadcast_to(a_w_dw[:, :, None, :], (N_A_LAYERS, CONV_WIDTH, SUBLANES, d))
    perm = jnp.arange(N_EXPERTS).reshape(N_GROUPS, EXPERTS_PER_GROUP).T.reshape(-1)
    rwt = r_w.T[perm]
    rbp = r_b[perm].reshape(N_EXPERTS, 1)
    wdkv = s_w_dkv.astype(BF16)
    wkr = _pad_rope(s_w_kr).astype(BF16)
    wkrs = _pad_rope(_swap_halves(s_w_kr)).astype(BF16)
    wukt = jnp.transpose(s_w_uk, (1, 2, 0)).astype(BF16)
    wuv = jnp.transpose(s_w_uv, (1, 0, 2)).astype(BF16)
    wdq = b_w_dq.astype(BF16)
    uq_pe = b_w_uq[..., QK_NOPE:]
    wuqn = b_w_uq[..., :QK_NOPE].reshape(-1, Q_LORA, N_HEADS * QK_NOPE).astype(BF16)
    wuqp = _pad_rope(uq_pe).reshape(-1, Q_LORA, N_HEADS * ROPE_PAD).astype(BF16)
    wuqps = _pad_rope(_swap_halves(uq_pe)).reshape(-1, Q_LORA, N_HEADS * ROPE_PAD).astype(BF16)
    wo = b_w_o.astype(BF16)

    cos_p, sin_p = _rope_tables(jnp.arange(sp))
    cos_s1, sin_s1 = _rope_tables(past_len + jnp.arange(ts))
    cos_s, sin_s = jnp.tile(cos_s1, (bs, 1)), jnp.tile(sin_s1, (bs, 1))

    cls_lo = jnp.array([g * EXPERTS_PER_GROUP + lo for g in range(N_GROUPS) for lo, _ in MEMBER_PAIRS], jnp.int32)
    cls_hi = jnp.array([g * EXPERTS_PER_GROUP + hi for g in range(N_GROUPS) for _, hi in MEMBER_PAIRS], jnp.int32)

    sorted_rows = []

    def moe_block(x, l, routed, kv_tables=None):
        x2d = x.reshape(-1, d)
        t = x2d.shape[0]
        ffn = (l, e_w1, e_w3, e_w2, _row(ln_ffn_g[l]), _row(ln_ffn_b[l]))
        gates_t, cls, pair, cnt = _router(x2d, rwt, rbp)
        if not routed:
            gates = gates_t.reshape(EXPERTS_PER_GROUP, N_GROUPS, -1).transpose(2, 1, 0).reshape(-1, N_EXPERTS)
            return _moe(x2d, gates, *ffn).reshape(x.shape)
        pos, tile_cls, n_tiles = _positions(cls, cnt)
        rows = t + N_CLASSES * MOE_TILE
        nt = n_tiles[0, :1]
        tile = jnp.arange(rows // MOE_TILE)
        tc = jnp.clip(tile_cls[0, jnp.minimum(tile, nt[0] - 1)], 0, N_CLASSES - 1)
        pos3 = pos.reshape(-1, 1, 2048)
        init = sorted_rows[0] if sorted_rows else jnp.zeros((rows, d + GATE_PAD), F32)
        xs = _row_scatter(pos3, x2d, pair, init)
        sorted_rows[:] = [xs]
        ys = _moe_sorted(cls_lo[tc], cls_hi[tc], nt, xs, *ffn)
        if kv_tables is not None:
            y, *kv = _row_gather_kv(pos3, ys, wdkv, _row(s_g_kv), wkr, wkrs, *kv_tables)
            return y.reshape(x.shape), kv
        return _row_gather(pos3, ys).reshape(x.shape)

    def conv_args(l):
        return (a_w_pw1b[l], _row(a_b_pw1[l]), w8[l], _row(a_b_dw[l]), _row(a_g_cn[l]), _row(a_b_cn[l]),
                a_w_pw2b[l], _row(a_b_pw2[l]), _row(ln_mix_g[l]), _row(ln_mix_b[l]))

    def q_args(j):
        return (wdq[j], _row(b_g_q[j]), wuqn[j], wuqp[j], wuqps[j], wukt)

    x = x_prompt
    conv_prompt = []
    for l in range(N_A_LAYERS):
        x, st = _conv_prompt(x, *conv_args(l))
        conv_prompt.append(st)
        if l < N_A_LAYERS - 1:
            x = moe_block(x, l, True)
        else:
            x, (ckv_p, kpe_p, ckvb_p, kpeb_p) = moe_block(x, l, True, kv_tables=(cos_p, sin_p))
    ckvb_p3, kpeb_p3 = ckvb_p.reshape(bp, sp, KV_LORA), kpeb_p.reshape(bp, sp, ROPE_PAD)
    for l in range(N_A_LAYERS, DEPTH):
        j = l - N_A_LAYERS
        x = _attn_prompt(x, ckvb_p3, kpeb_p3, cos_p, sin_p, *q_args(j), wuv, wo[j],
                         _row(ln_mix_g[l]), _row(ln_mix_b[l]))
        x = moe_block(x, l, True)
    y_prompt = x

    x = x_sample
    conv_sample = []
    for l in range(N_A_LAYERS):
        x, st = _conv_sample(x, state_conv[l], *conv_args(l))
        conv_sample.append(st)
        x = moe_block(x, l, False)
    ckv_s, kpe_s, ckvb_s, kpeb_s = _shared_kv(x.reshape(-1, d), wdkv, _row(s_g_kv), wkr, wkrs, cos_s, sin_s,
                                              tm=512, table_blocks=(bs * ts) // 512)
    cache_kpe_t = jnp.transpose(cache_kpe, (0, 2, 1))
    new_rows = LANES
    cnew =jnp.pad(ckvb_s.reshape(bs, ts, KV_LORA), ((0, 0), (0, new_rows - ts), (0, 0)))
    pnew = jnp.pad(kpeb_s.reshape(bs, ts, ROPE_PAD), ((0, 0), (0, new_rows - ts), (0, 0)))
    for l in range(N_A_LAYERS, DEPTH):
        j = l - N_A_LAYERS
        ql, qp = _q_sample(x, cos_s, sin_s, *q_args(j))
        o_lat = _attn_sample(page_table, ql.reshape(bs, N_HEADS * ts, KV_LORA),
                             qp.reshape(bs, N_HEADS * ts, ROPE_PAD), cnew, pnew, cache_ckv, cache_kpe_t)
        x = _o_sample(o_lat.reshape(bs, N_HEADS, ts, KV_LORA), x, wuv, wo[j],
                      _row(ln_mix_g[l]), _row(ln_mix_b[l]))
        x = moe_block(x, l, False)
    y_sample = x

    return (y_prompt, y_sample, jnp.stack(conv_prompt), jnp.stack(conv_sample),
            ckv_p.reshape(bp, sp, KV_LORA), kpe_p.reshape(bp, sp, QK_ROPE),
            ckv_s.reshape(bs, ts, KV_LORA), kpe_s.reshape(bs, ts, QK_ROPE))
```
